```python
import jax, jax.numpy as jnp
from jax import lax
import numpy as np

D_MODEL = 2048
BATCH = 2
SEQ = 4096
DEPTH = 1
DEC_BATCH = 16
DEC_SEQ = 2048
PAST_LEN = 128

MLA_HEADS = 8
MLA_NOPE = 128
MLA_ROPE = 64
MLA_V = 128
KV_RANK = 512
ROPE_THETA = 10000.0
SWA_HEADS = 8
SWA_KV_HEADS = 2
SWA_HEAD_DIM = 128
WINDOW = 128
BLOCK = 128
D_FF = -(-8 * D_MODEL // (3 * 256)) * 256
EPS = 1e-6

N_MLA_Q = MLA_HEADS * (MLA_NOPE + MLA_ROPE)
N_SWA_Q = SWA_HEADS * SWA_HEAD_DIM
N_SWA_KV = SWA_KV_HEADS * SWA_HEAD_DIM
SPLITS = (N_MLA_Q, KV_RANK, MLA_ROPE, N_SWA_Q, N_SWA_KV, N_SWA_KV)
D_IN = sum(SPLITS)
MLA_OUT = MLA_HEADS * MLA_V
SWA_OUT = SWA_HEADS * SWA_HEAD_DIM
D_MIX = MLA_OUT + SWA_OUT

kernel_name = "hymba_mla_swa_sandwich_encoder"


def rmsnorm(x, g):
    xf = x.astype(jnp.float32)
    var = jnp.mean(xf * xf, axis=-1, keepdims=True)
    return (xf * lax.rsqrt(var + EPS) * g.astype(jnp.float32)).astype(x.dtype)


def rope(x):
    S = x.shape[1]
    inv = ROPE_THETA ** (-jnp.arange(0, MLA_ROPE, 2, dtype=jnp.float32) / MLA_ROPE)
    ang = jnp.arange(S, dtype=jnp.float32)[:, None] * inv[None, :]
    c = jnp.cos(ang)[None, :, None, :]
    s = jnp.sin(ang)[None, :, None, :]
    x1, x2 = jnp.split(x.astype(jnp.float32), 2, axis=-1)
    return jnp.concatenate([x1 * c - x2 * s, x2 * c + x1 * s], axis=-1).astype(x.dtype)


def mla(q_all, ckv, kpe, g_ckv, w_uk, w_uv):
    B, S, _ = q_all.shape
    q = q_all.reshape(B, S, MLA_HEADS, MLA_NOPE + MLA_ROPE)
    q_nope = q[..., :MLA_NOPE]
    q_pe = rope(q[..., MLA_NOPE:])
    c_kv = rmsnorm(ckv, g_ckv)
    k_pe = rope(kpe[:, :, None, :])[:, :, 0, :]
    k_nope = jnp.einsum('bsr,rhd->bshd', c_kv, w_uk)
    v = jnp.einsum('bsr,rhd->bshd', c_kv, w_uv)
    scale = (MLA_NOPE + MLA_ROPE) ** -0.5
    nb = S // BLOCK
    qn_b = q_nope.reshape(B, nb, BLOCK, MLA_HEADS, MLA_NOPE).transpose(1, 0, 2, 3, 4)
    qp_b = q_pe.reshape(B, nb, BLOCK, MLA_HEADS, MLA_ROPE).transpose(1, 0, 2, 3, 4)

    def query_block(args):
        qn, qp = args
        s = (jnp.einsum('bqhd,bkhd->bhqk', qn, k_nope)
             + jnp.einsum('bqhd,bkd->bhqk', qp, k_pe))
        p = jax.nn.softmax(s.astype(jnp.float32) * scale, axis=-1).astype(v.dtype)
        return jnp.einsum('bhqk,bkhd->bqhd', p, v)

    o = lax.map(query_block, (qn_b, qp_b))
    return o.transpose(1, 0, 2, 3, 4).reshape(B, S, MLA_OUT)


def swa(q_all, k_all, v_all, sink):
    B, S, _ = q_all.shape
    G = SWA_HEADS // SWA_KV_HEADS
    D = SWA_HEAD_DIM
    nb = S // BLOCK
    qb = q_all.reshape(B, nb, BLOCK, SWA_KV_HEADS, G, D)
    pad = ((0, 0), (BLOCK, BLOCK), (0, 0), (0, 0))
    kp = jnp.pad(k_all.reshape(B, S, SWA_KV_HEADS, D), pad).reshape(B, nb + 2, BLOCK, SWA_KV_HEADS, D)
    vp = jnp.pad(v_all.reshape(B, S, SWA_KV_HEADS, D), pad).reshape(B, nb + 2, BLOCK, SWA_KV_HEADS, D)
    kb = jnp.concatenate([kp[:, :-2], kp[:, 1:-1], kp[:, 2:]], axis=2)
    vb = jnp.concatenate([vp[:, :-2], vp[:, 1:-1], vp[:, 2:]], axis=2)
    s = jnp.einsum('bnqkgd,bnrkd->bnkgqr', qb, kb).astype(jnp.float32) * (D ** -0.5)
    dist = jnp.abs(jnp.arange(BLOCK)[:, None] + BLOCK - jnp.arange(3 * BLOCK)[None, :])
    key_pos = jnp.arange(nb)[:, None] * BLOCK - BLOCK + jnp.arange(3 * BLOCK)[None, :]
    mask = (dist <= WINDOW)[None] & ((key_pos >= 0) & (key_pos < S))[:, None, :]
    slopes = 2.0 ** (-8.0 * jnp.arange(1, SWA_HEADS + 1, dtype=jnp.float32) / SWA_HEADS)
    bias = -slopes.reshape(SWA_KV_HEADS, G)[:, :, None, None] * dist.astype(jnp.float32)
    s = jnp.where(mask[None, :, None, None], s + bias, -jnp.inf)
    sink_b = sink.astype(jnp.float32).reshape(SWA_KV_HEADS, G)[None, None, :, :, None, None]
    m = jnp.maximum(jnp.max(s, axis=-1, keepdims=True), sink_b)
    e = jnp.exp(s - m)
    p = e / (jnp.sum(e, axis=-1, keepdims=True) + jnp.exp(sink_b - m))
    o = jnp.einsum('bnkgqr,bnrkd->bnqkgd', p.astype(vb.dtype), vb)
    return o.reshape(B, S, SWA_OUT)


def encoder_layer(x, g_pre_attn, w_in, g_ckv, w_uk, w_uv, sink, g_out_mla, g_out_swa, w_o,
                  g_post_attn, g_pre_ffn, w_gate, w_up, w_down, g_post_ffn):
    h = rmsnorm(x, g_pre_attn)
    proj = h @ w_in
    offs = np.cumsum(SPLITS)[:-1].tolist()
    q_mla, ckv, kpe, q_swa, k_swa, v_swa = jnp.split(proj, offs, axis=-1)
    o_a = mla(q_mla, ckv, kpe, g_ckv, w_uk, w_uv)
    o_b = swa(q_swa, k_swa, v_swa, sink)
    o = jnp.concatenate([rmsnorm(o_a, g_out_mla), rmsnorm(o_b, g_out_swa)], axis=-1) @ w_o
    x = x + rmsnorm(o, g_post_attn)
    h = rmsnorm(x, g_pre_ffn)
    f = (jax.nn.silu(h @ w_gate) * (h @ w_up)) @ w_down
    return x + rmsnorm(f, g_post_ffn)


def trunk(x, g_pre_attn, w_in, g_ckv, w_uk, w_uv, sink, g_out_mla, g_out_swa, w_o,
          g_post_attn, g_pre_ffn, w_gate, w_up, w_down, g_post_ffn):
    for l in range(DEPTH):
        x = encoder_layer(x, g_pre_attn[l], w_in[l], g_ckv[l], w_uk[l], w_uv[l], sink[l],
                          g_out_mla[l], g_out_swa[l], w_o[l], g_post_attn[l], g_pre_ffn[l],
                          w_gate[l], w_up[l], w_down[l], g_post_ffn[l])
    return x


def setup_inputs(seed: int = 0) -> dict:
    key = jax.random.key(seed)
    ks = jax.random.split(key, 20)
    f32 = jnp.float32

    def w(k, shape, fan_in):
        return jax.random.normal(k, shape, f32) * (fan_in ** -0.5)

    def gain(k, n):
        return 1.0 + 0.05 * jax.random.normal(k, (DEPTH, n), f32)

    return {
        "x_prompt": jax.random.normal(ks[0], (BATCH, SEQ, D_MODEL), f32),
        "x_sample": jax.random.normal(ks[1], (DEC_BATCH, DEC_SEQ, D_MODEL), f32),
        "g_pre_attn": gain(ks[2], D_MODEL),
        "w_in": w(ks[3], (DEPTH, D_MODEL, D_IN), D_MODEL),
        "g_ckv": gain(ks[4], KV_RANK),
        "w_uk": w(ks[5], (DEPTH, KV_RANK, MLA_HEADS, MLA_NOPE), KV_RANK),
        "w_uv": w(ks[6], (DEPTH, KV_RANK, MLA_HEADS, MLA_V), KV_RANK),
        "sink": 0.5 * jax.random.normal(ks[7], (DEPTH, SWA_HEADS), f32),
        "g_out_mla": gain(ks[8], MLA_OUT),
        "g_out_swa": gain(ks[9], SWA_OUT),
        "w_o": w(ks[10], (DEPTH, D_MIX, D_MODEL), D_MIX),
        "g_post_attn": gain(ks[11], D_MODEL),
        "g_pre_ffn": gain(ks[12], D_MODEL),
        "w_gate": w(ks[13], (DEPTH, D_MODEL, D_FF), D_MODEL),
        "w_up": w(ks[14], (DEPTH, D_MODEL, D_FF), D_MODEL),
        "w_down": w(ks[15], (DEPTH, D_FF, D_MODEL), D_FF),
        "g_post_ffn": gain(ks[16], D_MODEL),
    }


def reference(x_prompt, x_sample, g_pre_attn, w_in, g_ckv, w_uk, w_uv, sink, g_out_mla, g_out_swa,
              w_o, g_post_attn, g_pre_ffn, w_gate, w_up, w_down, g_post_ffn):
    y_prompt = trunk(x_prompt, g_pre_attn, w_in, g_ckv, w_uk, w_uv, sink, g_out_mla, g_out_swa, w_o,
                     g_post_attn, g_pre_ffn, w_gate, w_up, w_down, g_post_ffn)
    y_sample = trunk(x_sample, g_pre_attn, w_in, g_ckv, w_uk, w_uv, sink, g_out_mla, g_out_swa, w_o,
                     g_post_attn, g_pre_ffn, w_gate, w_up, w_down, g_post_ffn)
    return (y_prompt, y_sample)
```

```python
import functools
import math

import jax
import jax.numpy as jnp
from jax import lax
from jax.experimental import pallas as pl
from jax.experimental.pallas import tpu as pltpu

D_MODEL = 2048
MLA_HEADS = 8
MLA_NOPE = 128
MLA_ROPE = 64
MLA_V = 128
KV_RANK = 512
ROPE_THETA = 10000.0
SWA_HEADS = 8
SWA_KV_HEADS = 2
SWA_HEAD_DIM = 128
WINDOW = 128
BLOCK = 128
D_FF = -(-8 * D_MODEL // (3 * 256)) * 256
EPS = 1e-6

N_MLA_Q = MLA_HEADS * (MLA_NOPE + MLA_ROPE)
N_SWA_Q = SWA_HEADS * SWA_HEAD_DIM
N_SWA_KV = SWA_KV_HEADS * SWA_HEAD_DIM
MLA_OUT = MLA_HEADS * MLA_V
SWA_OUT = SWA_HEADS * SWA_HEAD_DIM

LANES = 128
QK_PAD = 2 * LANES
N_QPAD = MLA_HEADS * QK_PAD
OFF_QS = N_QPAD
OFF_KVS = OFF_QS + N_SWA_Q
OFF_CKV = OFF_KVS + 2 * N_SWA_KV
OFF_KPE = OFF_CKV + KV_RANK
N_PROJ = OFF_KPE + LANES

LOG2E = math.log2(math.e)
MLA_SCALE = (MLA_NOPE + MLA_ROPE) ** -0.5
SWA_SCALE = SWA_HEAD_DIM ** -0.5

VMEM_LIMIT = 56 * 1024 * 1024

TM_PROJ = 512
TQ_MLA = 256
TM_OUT = 512
TM_FFN = 512
TF_FFN = 512

_NT = (((1,), (1,)), ((), ()))


def _params(n_axes):
    return pltpu.CompilerParams(
        dimension_semantics=("arbitrary",) * n_axes, vmem_limit_bytes=VMEM_LIMIT)


def _resident(shape):
    return pl.BlockSpec(shape, lambda *_: (0,) * len(shape), pipeline_mode=pl.Buffered(1))


def _rms(x, g):
    var = jnp.mean(x * x, axis=-1, keepdims=True)
    return x * lax.rsqrt(var + EPS) * g


def _rope128(x, cos_t, sin_a, sin_b):
    return x * cos_t + pltpu.roll(x, 96, 1) * sin_a + pltpu.roll(x, 32, 1) * sin_b


def _proj_kernel(x_ref, g_ref, w1_ref, gckv_ref, wuk_ref, wuvt_ref, cos_ref, sina_ref, sinb_ref,
                 q_ref, k_ref, vt_ref, qs_ref, ks_ref, vs_ref):
    h = _rms(x_ref[...], g_ref[...]).astype(jnp.bfloat16)
    cos_t, sin_a, sin_b = cos_ref[...], sina_ref[...], sinb_ref[...]

    def proj(lo, hi):
        return jnp.dot(h, w1_ref[:, lo:hi], preferred_element_type=jnp.float32)

    q_scale = MLA_SCALE * LOG2E
    for hd in range(MLA_HEADS):
        pj = proj(hd * QK_PAD, (hd + 1) * QK_PAD) * q_scale
        q_ref[:, hd * QK_PAD:hd * QK_PAD + LANES] = pj[:, :LANES].astype(jnp.bfloat16)
        q_ref[:, hd * QK_PAD + LANES:(hd + 1) * QK_PAD] = _rope128(
            pj[:, LANES:], cos_t, sin_a, sin_b).astype(jnp.bfloat16)

    qs_ref[...] = proj(OFF_QS, OFF_KVS).astype(jnp.bfloat16)
    kvs = proj(OFF_KVS, OFF_CKV)
    ks_ref[...] = kvs[:, :N_SWA_KV].astype(jnp.bfloat16)
    vs_ref[...] = kvs[:, N_SWA_KV:].astype(jnp.bfloat16)

    lat = proj(OFF_CKV, N_PROJ)
    c_kv = _rms(lat[:, :KV_RANK], gckv_ref[...]).astype(jnp.bfloat16)
    kpe = _rope128(lat[:, KV_RANK:], cos_t, sin_a, sin_b).astype(jnp.bfloat16)
    k_nope = jnp.dot(c_kv, wuk_ref[...], preferred_element_type=jnp.float32)
    for hd in range(MLA_HEADS):
        k_ref[:, hd * QK_PAD:hd * QK_PAD + LANES] = (
            k_nope[:, hd * MLA_NOPE:(hd + 1) * MLA_NOPE].astype(jnp.bfloat16))
        k_ref[:, hd * QK_PAD + LANES:(hd + 1) * QK_PAD] = kpe
    vt_ref[...] = lax.dot_general(wuvt_ref[...], c_kv, _NT,
                                  preferred_element_type=jnp.float32).astype(jnp.bfloat16)


def _projection(x2d, B, S, g_pre, w1, g_ckv, w_uk, w_uvt, cos_t, sin_a, sin_b):
    T = B * S
    tm = TM_PROJ
    spt = S // tm
    row = lambda i: (i, 0)
    tab = pl.BlockSpec((tm, LANES), lambda i: (i % spt, 0))
    bf = jnp.bfloat16
    return pl.pallas_call(
        _proj_kernel,
        grid=(T // tm,),
        in_specs=[
            pl.BlockSpec((tm, D_MODEL), row),
            _resident((1, D_MODEL)),
            _resident((D_MODEL, N_PROJ)),
            _resident((1, KV_RANK)),
            _resident((KV_RANK, MLA_HEADS * MLA_NOPE)),
            _resident((MLA_OUT, KV_RANK)),
            tab, tab, tab,
        ],
        out_specs=[
            pl.BlockSpec((tm, N_QPAD), row),
            pl.BlockSpec((tm, N_QPAD), row),
            pl.BlockSpec((None, MLA_OUT, tm), lambda i: (i // spt, 0, i % spt)),
            pl.BlockSpec((tm, N_SWA_Q), row),
            pl.BlockSpec((tm, N_SWA_KV), row),
            pl.BlockSpec((tm, N_SWA_KV), row),
        ],
        out_shape=[
            jax.ShapeDtypeStruct((T, N_QPAD), bf),
            jax.ShapeDtypeStruct((T, N_QPAD), bf),
            jax.ShapeDtypeStruct((B, MLA_OUT, S), bf),
            jax.ShapeDtypeStruct((T, N_SWA_Q), bf),
            jax.ShapeDtypeStruct((T, N_SWA_KV), bf),
            jax.ShapeDtypeStruct((T, N_SWA_KV), bf),
        ],
        compiler_params=_params(1),
        name="proj",
    )(x2d, g_pre, w1, g_ckv, w_uk, w_uvt, cos_t, sin_a, sin_b)


def _mla_kernel(q_ref, k_ref, vt_ref, o_ref):
    s = lax.dot_general(k_ref[...], q_ref[...], _NT, preferred_element_type=jnp.float32)
    m = jnp.max(s, axis=0, keepdims=True)
    p = jnp.exp2(s - m)
    l = jnp.sum(p, axis=0, keepdims=True)
    acc = jnp.dot(vt_ref[...], p.astype(jnp.bfloat16), preferred_element_type=jnp.float32)
    o_ref[...] = (acc / l).T.astype(o_ref.dtype)


def _mla_attention(q, k, vt, B, S):
    T = B * S
    tq = TQ_MLA
    nq = S // tq
    return pl.pallas_call(
        _mla_kernel,
        grid=(B, MLA_HEADS, nq),
        in_specs=[
            pl.BlockSpec((tq, QK_PAD), lambda b, h, i: (b * nq + i, h)),
            pl.BlockSpec((S, QK_PAD), lambda b, h, i: (b, h)),
            pl.BlockSpec((None, MLA_V, S), lambda b, h, i: (b, h, 0)),
        ],
        out_specs=pl.BlockSpec((tq, MLA_V), lambda b, h, i: (b * nq + i, h)),
        out_shape=jax.ShapeDtypeStruct((T, MLA_OUT), jnp.bfloat16),
        compiler_params=_params(3),
        name="mla",
    )(q, k, vt)


def _swa_kernel(sink_ref, q_ref, kp_ref, kc_ref, kn_ref, vp_ref, vc_ref, vn_ref, o_ref, *, seq):
    j = pl.program_id(1)
    kcat = jnp.concatenate([kp_ref[...], kc_ref[...], kn_ref[...]], axis=0)
    vcat = jnp.concatenate([vp_ref[...], vc_ref[...], vn_ref[...]], axis=0)
    qi = lax.broadcasted_iota(jnp.int32, (BLOCK, 3 * BLOCK), 0)
    r = lax.broadcasted_iota(jnp.int32, (BLOCK, 3 * BLOCK), 1)
    dist = jnp.abs(qi + BLOCK - r)
    key_pos = j * BLOCK - BLOCK + r
    valid = (dist <= WINDOW) & (key_pos >= 0) & (key_pos < seq)
    distf = dist.astype(jnp.float32)
    group = SWA_HEADS // SWA_KV_HEADS
    for kv in range(SWA_KV_HEADS):
        kk = kcat[:, kv * SWA_HEAD_DIM:(kv + 1) * SWA_HEAD_DIM]
        vv = vcat[:, kv * SWA_HEAD_DIM:(kv + 1) * SWA_HEAD_DIM]
        for g in range(group):
            hd = kv * group + g
            slope = 2.0 ** (-8.0 * (hd + 1) / SWA_HEADS)
            sink = sink_ref[hd]
            qh = q_ref[:, hd * SWA_HEAD_DIM:(hd + 1) * SWA_HEAD_DIM]
            s = lax.dot_general(qh, kk, _NT, preferred_element_type=jnp.float32) * SWA_SCALE
            s = jnp.where(valid, s - slope * distf, -jnp.inf)
            m = jnp.maximum(jnp.max(s, axis=-1, keepdims=True), sink)
            e = jnp.exp(s - m)
            denom = jnp.sum(e, axis=-1, keepdims=True) + jnp.exp(sink - m)
            p = (e / denom).astype(jnp.bfloat16)
            o = jnp.dot(p, vv, preferred_element_type=jnp.float32)
            o_ref[:, hd * SWA_HEAD_DIM:(hd + 1) * SWA_HEAD_DIM] = o.astype(o_ref.dtype)


def _swa_attention(sink, qs, ks, vs, B, S):
    T = B * S
    nb = S // BLOCK
    cur = lambda b, j: (b * nb + j, 0)
    prev = lambda b, j: (b * nb + jnp.maximum(j - 1, 0), 0)
    nxt = lambda b, j: (b * nb + jnp.minimum(j + 1, nb - 1), 0)
    kvspec = lambda im: pl.BlockSpec((BLOCK, N_SWA_KV), im)
    return pl.pallas_call(
        functools.partial(_swa_kernel, seq=S),
        grid=(B, nb),
        in_specs=[
            pl.BlockSpec(memory_space=pltpu.SMEM),
            pl.BlockSpec((BLOCK, N_SWA_Q), cur),
            kvspec(prev), kvspec(cur), kvspec(nxt),
            kvspec(prev), kvspec(cur), kvspec(nxt),
        ],
        out_specs=pl.BlockSpec((BLOCK, SWA_OUT), cur),
        out_shape=jax.ShapeDtypeStruct((T, SWA_OUT), jnp.bfloat16),
        compiler_params=_params(2),
        name="swa",
    )(sink, qs, ks, ks, ks, vs, vs, vs)


def _out_kernel(oa_ref, ob_ref, x_ref, ga_ref, gb_ref, wo_ref, gpost_ref, gffn_ref, x1_ref, h2_ref):
    na = _rms(oa_ref[...].astype(jnp.float32), ga_ref[...]).astype(jnp.bfloat16)
    nb = _rms(ob_ref[...].astype(jnp.float32), gb_ref[...]).astype(jnp.bfloat16)
    o = jnp.dot(jnp.concatenate([na, nb], axis=1), wo_ref[...],
                preferred_element_type=jnp.float32)
    x1 = x_ref[...] + _rms(o, gpost_ref[...])
    x1_ref[...] = x1
    h2_ref[...] = _rms(x1, gffn_ref[...]).astype(jnp.bfloat16)


def _out_projection(oa, ob, x2d, g_a, g_b, w_o, g_post, g_ffn):
    T = x2d.shape[0]
    tm = TM_OUT
    row = lambda i: (i, 0)
    return pl.pallas_call(
        _out_kernel,
        grid=(T // tm,),
        in_specs=[
            pl.BlockSpec((tm, MLA_OUT), row),
            pl.BlockSpec((tm, SWA_OUT), row),
            pl.BlockSpec((tm, D_MODEL), row),
            _resident((1, MLA_OUT)),
            _resident((1, SWA_OUT)),
            _resident((MLA_OUT + SWA_OUT, D_MODEL)),
            _resident((1, D_MODEL)),
            _resident((1, D_MODEL)),
        ],
        out_specs=[pl.BlockSpec((tm, D_MODEL), row), pl.BlockSpec((tm, D_MODEL), row)],
        out_shape=[jax.ShapeDtypeStruct((T, D_MODEL), jnp.float32),
                   jax.ShapeDtypeStruct((T, D_MODEL), jnp.bfloat16)],
        compiler_params=_params(1),
        name="outproj",
    )(oa, ob, x2d, g_a, g_b, w_o, g_post, g_ffn)


def _ffn_kernel(h_ref, x1_ref, wg_ref, wu_ref, wd_ref, g_ref, y_ref, acc_ref):
    j = pl.program_id(1)
    h = h_ref[...]
    gate = jnp.dot(h, wg_ref[...], preferred_element_type=jnp.float32)
    up = jnp.dot(h, wu_ref[...], preferred_element_type=jnp.float32)
    act = (gate * jax.nn.sigmoid(gate) * up).astype(jnp.bfloat16)
    part = jnp.dot(act, wd_ref[...], preferred_element_type=jnp.float32)

    @pl.when(j == 0)
    def _():
        acc_ref[...] = part

    @pl.when(j > 0)
    def _():
        acc_ref[...] += part

    @pl.when(j == pl.num_programs(1) - 1)
    def _():
        y_ref[...] = x1_ref[...] + _rms(acc_ref[...], g_ref[...])


def _ffn(h2, x1, w_gate, w_up, w_down, g_post):
    T = h2.shape[0]
    tm, tf = TM_FFN, TF_FFN
    row = lambda i, j: (i, 0)
    return pl.pallas_call(
        _ffn_kernel,
        grid=(T // tm, D_FF // tf),
        in_specs=[
            pl.BlockSpec((tm, D_MODEL), row),
            pl.BlockSpec((tm, D_MODEL), row),
            pl.BlockSpec((D_MODEL, tf), lambda i, j: (0, j)),
            pl.BlockSpec((D_MODEL, tf), lambda i, j: (0, j)),
            pl.BlockSpec((tf, D_MODEL), lambda i, j: (j, 0)),
            _resident((1, D_MODEL)),
        ],
        out_specs=pl.BlockSpec((tm, D_MODEL), row),
        out_shape=jax.ShapeDtypeStruct((T, D_MODEL), jnp.float32),
        scratch_shapes=[pltpu.VMEM((tm, D_MODEL), jnp.float32)],
        compiler_params=_params(2),
        name="ffn",
    )(h2, x1, w_gate, w_up, w_down, g_post)


def _rope_tables(S):
    inv = ROPE_THETA ** (-jnp.arange(0, MLA_ROPE, 2, dtype=jnp.float32) / MLA_ROPE)
    ang = jnp.arange(S, dtype=jnp.float32)[:, None] * inv[None, :]
    c, s = jnp.cos(ang), jnp.sin(ang)
    z = jnp.zeros_like(c)
    cos_t = jnp.concatenate([c, c, z, z], axis=1)
    sin_a = jnp.concatenate([-s, z, z, z], axis=1)
    sin_b = jnp.concatenate([z, s, z, z], axis=1)
    return cos_t, sin_a, sin_b


def _prep_weights(w_in, w_uk, w_uv, w_o, w_gate, w_up, w_down):
    bf = jnp.bfloat16
    wq = w_in[:, :N_MLA_Q].reshape(D_MODEL, MLA_HEADS, MLA_NOPE + MLA_ROPE)
    wq = jnp.pad(wq, ((0, 0), (0, 0), (0, QK_PAD - MLA_NOPE - MLA_ROPE))).reshape(D_MODEL, N_QPAD)
    o = N_MLA_Q
    w_ckv = w_in[:, o:o + KV_RANK]; o += KV_RANK
    w_kpe = jnp.pad(w_in[:, o:o + MLA_ROPE], ((0, 0), (0, LANES - MLA_ROPE))); o += MLA_ROPE
    w_swa = w_in[:, o:]
    w1 = jnp.concatenate([wq, w_swa, w_ckv, w_kpe], axis=1).astype(bf)
    return dict(
        w1=w1,
        w_uk=w_uk.reshape(KV_RANK, MLA_HEADS * MLA_NOPE).astype(bf),
        w_uvt=w_uv.reshape(KV_RANK, MLA_OUT).T.astype(bf),
        w_o=w_o.astype(bf), w_gate=w_gate.astype(bf), w_up=w_up.astype(bf),
        w_down=w_down.astype(bf))


def _layer(x, w, g_pre_attn, g_ckv, sink, g_out_mla, g_out_swa, g_post_attn, g_pre_ffn, g_post_ffn):
    B, S, _ = x.shape
    x2d = x.reshape(B * S, D_MODEL)
    cos_t, sin_a, sin_b = _rope_tables(S)
    q, k, vt, qs, ks, vs = _projection(x2d, B, S, g_pre_attn, w["w1"], g_ckv, w["w_uk"],
                                       w["w_uvt"], cos_t, sin_a, sin_b)
    oa = _mla_attention(q, k, vt, B, S)
    ob = _swa_attention(sink, qs, ks, vs, B, S)
    x1, h2 = _out_projection(oa, ob, x2d, g_out_mla, g_out_swa, w["w_o"], g_post_attn, g_pre_ffn)
    y = _ffn(h2, x1, w["w_gate"], w["w_up"], w["w_down"], g_post_ffn)
    return y.reshape(B, S, D_MODEL)


def kernel(x_prompt, x_sample, g_pre_attn, w_in, g_ckv, w_uk, w_uv, sink, g_out_mla, g_out_swa,
           w_o, g_post_attn, g_pre_ffn, w_gate, w_up, w_down, g_post_ffn):
    assert w_in.shape[0] == 1, "single layer"
    w = _prep_weights(w_in[0], w_uk[0], w_uv[0], w_o[0], w_gate[0], w_up[0], w_down[0])
    gains = (g_pre_attn, g_ckv[0:1], sink[0], g_out_mla, g_out_swa, g_post_attn, g_pre_ffn,
             g_post_ffn)
    return (_layer(x_prompt, w, *gains), _layer(x_sample, w, *gains))
```

```python
import functools
import math

import jax
import jax.numpy as jnp
from jax import lax
from jax.experimental import pallas as pl
from jax.experimental.pallas import tpu as pltpu

D_MODEL = 2048
MLA_HEADS = 8
MLA_NOPE = 128
MLA_ROPE = 64
MLA_V = 128
KV_RANK = 512
ROPE_THETA = 10000.0
SWA_HEADS = 8
SWA_KV_HEADS = 2
SWA_HEAD_DIM = 128
WINDOW = 128
BLOCK = 128
D_FF = -(-8 * D_MODEL // (3 * 256)) * 256
EPS = 1e-6

N_MLA_Q = MLA_HEADS * (MLA_NOPE + MLA_ROPE)
N_SWA_Q = SWA_HEADS * SWA_HEAD_DIM
N_SWA_KV = SWA_KV_HEADS * SWA_HEAD_DIM
MLA_OUT = MLA_HEADS * MLA_V
SWA_OUT = SWA_HEADS * SWA_HEAD_DIM

LANES = 128
QK_PAD = 2 * LANES
N_QPAD = MLA_HEADS * QK_PAD
OFF_QS = N_QPAD
OFF_KVS = OFF_QS + N_SWA_Q
OFF_CKV = OFF_KVS + 2 * N_SWA_KV
OFF_KPE = OFF_CKV + KV_RANK
N_PROJ = OFF_KPE + LANES

LOG2E = math.log2(math.e)
MLA_SCALE = (MLA_NOPE + MLA_ROPE) ** -0.5
SWA_SCALE = SWA_HEAD_DIM ** -0.5

VMEM_LIMIT = 56 * 1024 * 1024

TM_PROJ = 512
TQ_MLA = 256
CK_MLA = 512
TM_OUT = 512
TM_FFN = 512
TF_FFN = 512

_NT = (((1,), (1,)), ((), ()))


def _params(n_axes):
    return pltpu.CompilerParams(
        dimension_semantics=("arbitrary",) * n_axes, vmem_limit_bytes=VMEM_LIMIT)


def _resident(shape):
    return pl.BlockSpec(shape, lambda *_: (0,) * len(shape), pipeline_mode=pl.Buffered(1))


def _rms(x, g):
    var = jnp.mean(x * x, axis=-1, keepdims=True)
    return x * lax.rsqrt(var + EPS) * g


def _rope128(x, cos_t, sin_a, sin_b):
    return x * cos_t + pltpu.roll(x, 96, 1) * sin_a + pltpu.roll(x, 32, 1) * sin_b


def _proj_kernel(x_ref, g_ref, w1_ref, gckv_ref, wuk_ref, wuvt_ref, cos_ref, sina_ref, sinb_ref,
                 q_ref, k_ref, vt_ref, qs_ref, ks_ref, vs_ref):
    h = _rms(x_ref[...], g_ref[...]).astype(jnp.bfloat16)
    cos_t, sin_a, sin_b = cos_ref[...], sina_ref[...], sinb_ref[...]

    def proj(lo, hi):
        return jnp.dot(h, w1_ref[:, lo:hi], preferred_element_type=jnp.float32)

    q_scale = MLA_SCALE * LOG2E
    for hd in range(MLA_HEADS):
        pj = proj(hd * QK_PAD, (hd + 1) * QK_PAD) * q_scale
        q_ref[hd, :, :LANES] = pj[:, :LANES].astype(jnp.bfloat16)
        q_ref[hd, :, LANES:] = _rope128(pj[:, LANES:], cos_t, sin_a, sin_b).astype(jnp.bfloat16)

    qs_ref[...] = proj(OFF_QS, OFF_KVS).astype(jnp.bfloat16)
    kvs = proj(OFF_KVS, OFF_CKV)
    ks_ref[...] = kvs[:, :N_SWA_KV].astype(jnp.bfloat16)
    vs_ref[...] = kvs[:, N_SWA_KV:].astype(jnp.bfloat16)

    lat = proj(OFF_CKV, N_PROJ)
    c_kv = _rms(lat[:, :KV_RANK], gckv_ref[...]).astype(jnp.bfloat16)
    kpe = _rope128(lat[:, KV_RANK:], cos_t, sin_a, sin_b).astype(jnp.bfloat16)
    k_nope = jnp.dot(c_kv, wuk_ref[...], preferred_element_type=jnp.float32)
    for hd in range(MLA_HEADS):
        k_ref[hd, :, :LANES] = k_nope[:, hd * MLA_NOPE:(hd + 1) * MLA_NOPE].astype(jnp.bfloat16)
        k_ref[hd, :, LANES:] = kpe
    vt_ref[...] = lax.dot_general(wuvt_ref[...], c_kv, _NT,
                                  preferred_element_type=jnp.float32).astype(jnp.bfloat16)


def _projection(x2d, B, S, g_pre, w1, g_ckv, w_uk, w_uvt, cos_t, sin_a, sin_b):
    T = B * S
    tm = TM_PROJ
    spt = S // tm
    row = lambda i: (i, 0)
    tab = pl.BlockSpec((tm, LANES), lambda i: (i % spt, 0))
    bf = jnp.bfloat16
    return pl.pallas_call(
        _proj_kernel,
        grid=(T // tm,),
        in_specs=[
            pl.BlockSpec((tm, D_MODEL), row),
            _resident((1, D_MODEL)),
            _resident((D_MODEL, N_PROJ)),
            _resident((1, KV_RANK)),
            _resident((KV_RANK, MLA_HEADS * MLA_NOPE)),
            _resident((MLA_OUT, KV_RANK)),
            tab, tab, tab,
        ],
        out_specs=[
            pl.BlockSpec((MLA_HEADS, tm, QK_PAD), lambda i: (0, i, 0)),
            pl.BlockSpec((MLA_HEADS, tm, QK_PAD), lambda i: (0, i, 0)),
            pl.BlockSpec((None, MLA_OUT, tm), lambda i: (i // spt, 0, i % spt)),
            pl.BlockSpec((tm, N_SWA_Q), row),
            pl.BlockSpec((tm, N_SWA_KV), row),
            pl.BlockSpec((tm, N_SWA_KV), row),
        ],
        out_shape=[
            jax.ShapeDtypeStruct((MLA_HEADS, T, QK_PAD), bf),
            jax.ShapeDtypeStruct((MLA_HEADS, T, QK_PAD), bf),
            jax.ShapeDtypeStruct((B, MLA_OUT, S), bf),
            jax.ShapeDtypeStruct((T, N_SWA_Q), bf),
            jax.ShapeDtypeStruct((T, N_SWA_KV), bf),
            jax.ShapeDtypeStruct((T, N_SWA_KV), bf),
        ],
        compiler_params=_params(1),
        name="proj",
    )(x2d, g_pre, w1, g_ckv, w_uk, w_uvt, cos_t, sin_a, sin_b)


def _mla_kernel(q_ref, k_ref, vt_ref, o_ref, sa_ref, sb_ref):
    seq = k_ref.shape[1]
    chunks = [slice(c * CK_MLA, (c + 1) * CK_MLA) for c in range(seq // CK_MLA)]

    def scores(hd, s_ref):
        q = q_ref[hd]
        m = None
        for ck in chunks:
            s = lax.dot_general(k_ref[hd, ck, :], q, _NT, preferred_element_type=jnp.float32)
            s_ref[ck, :] = s
            cm = jnp.max(s, axis=0, keepdims=True)
            m = cm if m is None else jnp.maximum(m, cm)
        return m

    def attend(hd, s_ref, m):
        l = jnp.zeros_like(m)
        acc = jnp.zeros((MLA_V, m.shape[1]), jnp.float32)
        for ck in chunks:
            p = jnp.exp2(s_ref[ck, :] - m)
            l = l + jnp.sum(p, axis=0, keepdims=True)
            acc = acc + jnp.dot(vt_ref[hd, :, ck], p.astype(jnp.bfloat16),
                                preferred_element_type=jnp.float32)
        o_ref[hd] = (acc / l).T.astype(o_ref.dtype)

    def body(j, m_a):
        m_b = scores(2 * j + 1, sb_ref)
        attend(2 * j, sa_ref, m_a)
        m_next = scores(2 * j + 2, sa_ref)
        attend(2 * j + 1, sb_ref, m_b)
        return m_next

    assert MLA_HEADS % 2 == 0
    m_a = lax.fori_loop(0, MLA_HEADS // 2 - 1, body, scores(0, sa_ref))
    m_b = scores(MLA_HEADS - 1, sb_ref)
    attend(MLA_HEADS - 2, sa_ref, m_a)
    attend(MLA_HEADS - 1, sb_ref, m_b)


def _mla_attention(q, k, vt, B, S):
    T = B * S
    tq = TQ_MLA
    nq = S // tq
    kv_bytes = S * (N_QPAD + MLA_OUT) * 2
    mode = dict(pipeline_mode=pl.Buffered(1)) if 2 * kv_bytes > VMEM_LIMIT // 2 else {}
    return pl.pallas_call(
        _mla_kernel,
        grid=(B, nq),
        in_specs=[
            pl.BlockSpec((MLA_HEADS, tq, QK_PAD), lambda b, i: (0, b * nq + i, 0)),
            pl.BlockSpec((MLA_HEADS, S, QK_PAD), lambda b, i: (0, b, 0), **mode),
            pl.BlockSpec((None, MLA_HEADS, MLA_V, S), lambda b, i: (b, 0, 0, 0), **mode),
        ],
        out_specs=pl.BlockSpec((MLA_HEADS, tq, MLA_V), lambda b, i: (0, b * nq + i, 0)),
        out_shape=jax.ShapeDtypeStruct((MLA_HEADS, T, MLA_V), jnp.bfloat16),
        scratch_shapes=[pltpu.VMEM((S, tq), jnp.float32), pltpu.VMEM((S, tq), jnp.float32)],
        compiler_params=_params(2),
        name="mla",
    )(q, k, vt.reshape(B, MLA_HEADS, MLA_V, S))


def _swa_kernel(sink_ref, q_ref, kp_ref, kc_ref, kn_ref, vp_ref, vc_ref, vn_ref, o_ref, *, seq):
    j = pl.program_id(1)
    kcat = jnp.concatenate([kp_ref[...], kc_ref[...], kn_ref[...]], axis=0)
    vcat = jnp.concatenate([vp_ref[...], vc_ref[...], vn_ref[...]], axis=0)
    qi = lax.broadcasted_iota(jnp.int32, (BLOCK, 3 * BLOCK), 0)
    r = lax.broadcasted_iota(jnp.int32, (BLOCK, 3 * BLOCK), 1)
    dist = jnp.abs(qi + BLOCK - r)
    key_pos = j * BLOCK - BLOCK + r
    valid = (dist <= WINDOW) & (key_pos >= 0) & (key_pos < seq)
    distf = dist.astype(jnp.float32)
    group = SWA_HEADS // SWA_KV_HEADS
    for kv in range(SWA_KV_HEADS):
        kk = kcat[:, kv * SWA_HEAD_DIM:(kv + 1) * SWA_HEAD_DIM]
        vv = vcat[:, kv * SWA_HEAD_DIM:(kv + 1) * SWA_HEAD_DIM]
        for g in range(group):
            hd = kv * group + g
            slope = 2.0 ** (-8.0 * (hd + 1) / SWA_HEADS)
            sink = sink_ref[hd]
            qh = q_ref[:, hd * SWA_HEAD_DIM:(hd + 1) * SWA_HEAD_DIM]
            s = lax.dot_general(qh, kk, _NT, preferred_element_type=jnp.float32) * SWA_SCALE
            s = jnp.where(valid, s - slope * distf, -jnp.inf)
            m = jnp.maximum(jnp.max(s, axis=-1, keepdims=True), sink)
            e = jnp.exp(s - m)
            denom = jnp.sum(e, axis=-1, keepdims=True) + jnp.exp(sink - m)
            p = (e / denom).astype(jnp.bfloat16)
            o = jnp.dot(p, vv, preferred_element_type=jnp.float32)
            o_ref[:, hd * SWA_HEAD_DIM:(hd + 1) * SWA_HEAD_DIM] = o.astype(o_ref.dtype)


def _swa_attention(sink, qs, ks, vs, B, S):
    T = B * S
    nb = S // BLOCK
    cur = lambda b, j: (b * nb + j, 0)
    prev = lambda b, j: (b * nb + jnp.maximum(j - 1, 0), 0)
    nxt = lambda b, j: (b * nb + jnp.minimum(j + 1, nb - 1), 0)
    kvspec = lambda im: pl.BlockSpec((BLOCK, N_SWA_KV), im)
    return pl.pallas_call(
        functools.partial(_swa_kernel, seq=S),
        grid=(B, nb),
        in_specs=[
            pl.BlockSpec(memory_space=pltpu.SMEM),
            pl.BlockSpec((BLOCK, N_SWA_Q), cur),
            kvspec(prev), kvspec(cur), kvspec(nxt),
            kvspec(prev), kvspec(cur), kvspec(nxt),
        ],
        out_specs=pl.BlockSpec((BLOCK, SWA_OUT), cur),
        out_shape=jax.ShapeDtypeStruct((T, SWA_OUT), jnp.bfloat16),
        compiler_params=_params(2),
        name="swa",
    )(sink, qs, ks, ks, ks, vs, vs, vs)


def _out_kernel(oa_ref, ob_ref, x_ref, ga_ref, gb_ref, wo_ref, gpost_ref, gffn_ref, x1_ref, h2_ref):
    oa = jnp.concatenate([oa_ref[hd] for hd in range(MLA_HEADS)], axis=1)
    na = _rms(oa.astype(jnp.float32), ga_ref[...]).astype(jnp.bfloat16)
    nb = _rms(ob_ref[...].astype(jnp.float32), gb_ref[...]).astype(jnp.bfloat16)
    o = jnp.dot(jnp.concatenate([na, nb], axis=1), wo_ref[...],
                preferred_element_type=jnp.float32)
    x1 = x_ref[...] + _rms(o, gpost_ref[...])
    x1_ref[...] = x1
    h2_ref[...] = _rms(x1, gffn_ref[...]).astype(jnp.bfloat16)


def _out_projection(oa, ob, x2d, g_a, g_b, w_o, g_post, g_ffn):
    T = x2d.shape[0]
    tm = TM_OUT
    row = lambda i: (i, 0)
    return pl.pallas_call(
        _out_kernel,
        grid=(T // tm,),
        in_specs=[
            pl.BlockSpec((MLA_HEADS, tm, MLA_V), lambda i: (0, i, 0)),
            pl.BlockSpec((tm, SWA_OUT), row),
            pl.BlockSpec((tm, D_MODEL), row),
            _resident((1, MLA_OUT)),
            _resident((1, SWA_OUT)),
            _resident((MLA_OUT + SWA_OUT, D_MODEL)),
            _resident((1, D_MODEL)),
            _resident((1, D_MODEL)),
        ],
        out_specs=[pl.BlockSpec((tm, D_MODEL), row), pl.BlockSpec((tm, D_MODEL), row)],
        out_shape=[jax.ShapeDtypeStruct((T, D_MODEL), jnp.float32),
                   jax.ShapeDtypeStruct((T, D_MODEL), jnp.bfloat16)],
        compiler_params=_params(1),
        name="outproj",
    )(oa, ob, x2d, g_a, g_b, w_o, g_post, g_ffn)


def _ffn_kernel(h_ref, x1_ref, wg_ref, wu_ref, wd_ref, g_ref, y_ref, acc_ref):
    j = pl.program_id(1)

    @pl.when(j == 0)
    def _():
        acc_ref[...] = jnp.zeros_like(acc_ref)

    h = h_ref[...]
    gate = jnp.dot(h, wg_ref[...], preferred_element_type=jnp.float32)
    up = jnp.dot(h, wu_ref[...], preferred_element_type=jnp.float32)
    act = (gate * jax.nn.sigmoid(gate) * up).astype(jnp.bfloat16)
    acc_ref[...] += jnp.dot(act, wd_ref[...], preferred_element_type=jnp.float32)

    @pl.when(j == pl.num_programs(1) - 1)
    def _():
        y_ref[...] = x1_ref[...] + _rms(acc_ref[...], g_ref[...])


def _ffn(h2, x1, w_gate, w_up, w_down, g_post):
    T = h2.shape[0]
    tm, tf = TM_FFN, TF_FFN
    row = lambda i, j: (i, 0)
    return pl.pallas_call(
        _ffn_kernel,
        grid=(T // tm, D_FF // tf),
        in_specs=[
            pl.BlockSpec((tm, D_MODEL), row),
            pl.BlockSpec((tm, D_MODEL), row),
            pl.BlockSpec((D_MODEL, tf), lambda i, j: (0, j)),
            pl.BlockSpec((D_MODEL, tf), lambda i, j: (0, j)),
            pl.BlockSpec((tf, D_MODEL), lambda i, j: (j, 0)),
            _resident((1, D_MODEL)),
        ],
        out_specs=pl.BlockSpec((tm, D_MODEL), row),
        out_shape=jax.ShapeDtypeStruct((T, D_MODEL), jnp.float32),
        scratch_shapes=[pltpu.VMEM((tm, D_MODEL), jnp.float32)],
        compiler_params=_params(2),
        name="ffn",
    )(h2, x1, w_gate, w_up, w_down, g_post)


def _rope_tables(S):
    inv = ROPE_THETA ** (-jnp.arange(0, MLA_ROPE, 2, dtype=jnp.float32) / MLA_ROPE)
    ang = jnp.arange(S, dtype=jnp.float32)[:, None] * inv[None, :]
    c, s = jnp.cos(ang), jnp.sin(ang)
    z = jnp.zeros_like(c)
    cos_t = jnp.concatenate([c, c, z, z], axis=1)
    sin_a = jnp.concatenate([-s, z, z, z], axis=1)
    sin_b = jnp.concatenate([z, s, z, z], axis=1)
    return cos_t, sin_a, sin_b


def _prep_weights(w_in, w_uk, w_uv, w_o, w_gate, w_up, w_down):
    bf = jnp.bfloat16
    wq = w_in[:, :N_MLA_Q].reshape(D_MODEL, MLA_HEADS, MLA_NOPE + MLA_ROPE)
    wq = jnp.pad(wq, ((0, 0), (0, 0), (0, QK_PAD - MLA_NOPE - MLA_ROPE))).reshape(D_MODEL, N_QPAD)
    o = N_MLA_Q
    w_ckv = w_in[:, o:o + KV_RANK]; o += KV_RANK
    w_kpe = jnp.pad(w_in[:, o:o + MLA_ROPE], ((0, 0), (0, LANES - MLA_ROPE))); o += MLA_ROPE
    w_swa = w_in[:, o:]
    w1 = jnp.concatenate([wq, w_swa, w_ckv, w_kpe], axis=1).astype(bf)
    return dict(
        w1=w1,
        w_uk=w_uk.reshape(KV_RANK, MLA_HEADS * MLA_NOPE).astype(bf),
        w_uvt=w_uv.reshape(KV_RANK, MLA_OUT).T.astype(bf),
        w_o=w_o.astype(bf), w_gate=w_gate.astype(bf), w_up=w_up.astype(bf),
        w_down=w_down.astype(bf))


def _layer(x, w, g_pre_attn, g_ckv, sink, g_out_mla, g_out_swa, g_post_attn, g_pre_ffn, g_post_ffn):
    B, S, _ = x.shape
    x2d = x.reshape(B * S, D_MODEL)
    cos_t, sin_a, sin_b = _rope_tables(S)
    q, k, vt, qs, ks, vs = _projection(x2d, B, S, g_pre_attn, w["w1"], g_ckv, w["w_uk"],
                                       w["w_uvt"], cos_t, sin_a, sin_b)
    oa = _mla_attention(q, k, vt, B, S)
    ob = _swa_attention(sink, qs, ks, vs, B, S)
    x1, h2 = _out_projection(oa, ob, x2d, g_out_mla, g_out_swa, w["w_o"], g_post_attn, g_pre_ffn)
    y = _ffn(h2, x1, w["w_gate"], w["w_up"], w["w_down"], g_post_ffn)
    return y.reshape(B, S, D_MODEL)


def kernel(x_prompt, x_sample, g_pre_attn, w_in, g_ckv, w_uk, w_uv, sink, g_out_mla, g_out_swa,
           w_o, g_post_attn, g_pre_ffn, w_gate, w_up, w_down, g_post_ffn):
    assert w_in.shape[0] == 1, "single layer"
    w = _prep_weights(w_in[0], w_uk[0], w_uv[0], w_o[0], w_gate[0], w_up[0], w_down[0])
    gains = (g_pre_attn, g_ckv[0:1], sink[0], g_out_mla, g_out_swa, g_post_attn, g_pre_ffn,
             g_post_ffn)
    return (_layer(x_prompt, w, *gains), _layer(x_sample, w, *gains))
```

```python
import functools
import math

import jax
import jax.numpy as jnp
from jax import lax
from jax.experimental import pallas as pl
from jax.experimental.pallas import tpu as pltpu

D_MODEL = 2048
MLA_HEADS = 8
MLA_NOPE = 128
MLA_ROPE = 64
MLA_V = 128
KV_RANK = 512
ROPE_THETA = 10000.0
SWA_HEADS = 8
SWA_KV_HEADS = 2
SWA_HEAD_DIM = 128
SWA_GROUP = SWA_HEADS // SWA_KV_HEADS
WINDOW = 128
BLOCK = 128
D_FF = -(-8 * D_MODEL // (3 * 256)) * 256
EPS = 1e-6

N_MLA_Q = MLA_HEADS * (MLA_NOPE + MLA_ROPE)
N_SWA_Q = SWA_HEADS * SWA_HEAD_DIM
N_SWA_KV = SWA_KV_HEADS * SWA_HEAD_DIM
MLA_OUT = MLA_HEADS * MLA_V
SWA_OUT = SWA_HEADS * SWA_HEAD_DIM

LANES = 128
QK_PAD = 2 * LANES
OFF_QPE = MLA_HEADS * MLA_NOPE
OFF_QS = OFF_QPE + MLA_HEADS * MLA_ROPE
OFF_KS = OFF_QS + N_SWA_Q
OFF_VS = OFF_KS + N_SWA_KV
OFF_CKV = OFF_VS + N_SWA_KV
OFF_KPE = OFF_CKV + KV_RANK
N_PROJ = OFF_KPE + LANES

LOG2E = math.log2(math.e)
MLA_SCALE = (MLA_NOPE + MLA_ROPE) ** -0.5
SWA_SCALE = SWA_HEAD_DIM ** -0.5

VMEM_LIMIT = 56 * 1024 * 1024

TM_PROJ = 512
TQ_MLA = 256
QT_MLA = 2
CK_MLA = 512
TQ_SWA = 512
TM_OUT = 512
TM_FFN = 1024
TF_FFN = 512
SUB_FFN = 512
NORM_ROWS = 128

_NT = (((1,), (1,)), ((), ()))


def _params(n_axes):
    return pltpu.CompilerParams(
        dimension_semantics=("arbitrary",) * n_axes, vmem_limit_bytes=VMEM_LIMIT)


def _resident(shape):
    return pl.BlockSpec(shape, lambda *_: (0,) * len(shape), pipeline_mode=pl.Buffered(1))


def _rms(x, g):
    var = jnp.mean(x * x, axis=-1, keepdims=True)
    return x * lax.rsqrt(var + EPS) * g


def _rope_pairs(x, cos_t, sin_a, sin_b):
    return x * cos_t + pltpu.roll(x, 96, 1) * sin_a + pltpu.roll(x, 32, 1) * sin_b


def _proj_kernel(x_ref, g_ref, w1_ref, gckv_ref, wuk_ref, wuvt_ref, cos_ref, sina_ref, sinb_ref,
                 q_ref, k_ref, vt_ref, qs_ref, ks_ref, vst_ref):
    h = _rms(x_ref[...], g_ref[...]).astype(jnp.bfloat16)
    cos_t, sin_a, sin_b = cos_ref[...], sina_ref[...], sinb_ref[...]
    bf = jnp.bfloat16

    def proj(lo, hi):
        return jnp.dot(h, w1_ref[:, lo:hi], preferred_element_type=jnp.float32)

    low_half = lax.broadcasted_iota(jnp.int32, cos_t.shape, 1) < MLA_ROPE
    q_scale = MLA_SCALE * LOG2E
    q_pe = proj(OFF_QPE, OFF_QS) * q_scale
    for pair in range(MLA_HEADS // 2):
        nope = proj(pair * 2 * MLA_NOPE, (pair + 1) * 2 * MLA_NOPE) * q_scale
        rot = _rope_pairs(q_pe[:, pair * LANES:(pair + 1) * LANES], cos_t, sin_a, sin_b)
        q_ref[2 * pair, :, :LANES] = nope[:, :MLA_NOPE].astype(bf)
        q_ref[2 * pair, :, LANES:] = jnp.where(low_half, rot, 0.0).astype(bf)
        q_ref[2 * pair + 1, :, :LANES] = nope[:, MLA_NOPE:].astype(bf)
        q_ref[2 * pair + 1, :, LANES:] = jnp.where(low_half, pltpu.roll(rot, 64, 1), 0.0).astype(bf)

    qs_ref[...] = proj(OFF_QS, OFF_KS).astype(bf)
    kvs = proj(OFF_KS, OFF_CKV)
    ks_ref[...] = kvs[:, :N_SWA_KV].astype(bf)
    vst_ref[...] = kvs[:, N_SWA_KV:].T.astype(bf)

    lat = proj(OFF_CKV, N_PROJ)
    c_kv = _rms(lat[:, :KV_RANK], gckv_ref[...]).astype(bf)
    kpe = _rope_pairs(lat[:, KV_RANK:], cos_t, sin_a, sin_b).astype(bf)
    k_nope = jnp.dot(c_kv, wuk_ref[...], preferred_element_type=jnp.float32)
    for hd in range(MLA_HEADS):
        k_ref[hd, :, :LANES] = k_nope[:, hd * MLA_NOPE:(hd + 1) * MLA_NOPE].astype(bf)
        k_ref[hd, :, LANES:] = kpe
    vt_ref[...] = lax.dot_general(wuvt_ref[...], c_kv, _NT,
                                  preferred_element_type=jnp.float32).astype(bf)


def _projection(x2d, B, S, g_pre, w1, g_ckv, w_uk, w_uvt, cos_t, sin_a, sin_b):
    T = B * S
    tm = TM_PROJ
    spt = S // tm
    row = lambda i: (i, 0)
    tab = pl.BlockSpec((tm, LANES), lambda i: (i % spt, 0))
    tposed = lambda n: pl.BlockSpec((None, n, tm), lambda i: (i // spt, 0, i % spt))
    bf = jnp.bfloat16
    return pl.pallas_call(
        _proj_kernel,
        grid=(T // tm,),
        in_specs=[
            pl.BlockSpec((tm, D_MODEL), row),
            _resident((1, D_MODEL)),
            _resident((D_MODEL, N_PROJ)),
            _resident((1, KV_RANK)),
            _resident((KV_RANK, MLA_HEADS * MLA_NOPE)),
            _resident((MLA_OUT, KV_RANK)),
            tab, tab, tab,
        ],
        out_specs=[
            pl.BlockSpec((MLA_HEADS, tm, QK_PAD), lambda i: (0, i, 0)),
            pl.BlockSpec((MLA_HEADS, tm, QK_PAD), lambda i: (0, i, 0)),
            tposed(MLA_OUT),
            pl.BlockSpec((tm, N_SWA_Q), row),
            pl.BlockSpec((tm, N_SWA_KV), row),
            tposed(N_SWA_KV),
        ],
        out_shape=[
            jax.ShapeDtypeStruct((MLA_HEADS, T, QK_PAD), bf),
            jax.ShapeDtypeStruct((MLA_HEADS, T, QK_PAD), bf),
            jax.ShapeDtypeStruct((B, MLA_OUT, S), bf),
            jax.ShapeDtypeStruct((T, N_SWA_Q), bf),
            jax.ShapeDtypeStruct((T, N_SWA_KV), bf),
            jax.ShapeDtypeStruct((B, N_SWA_KV, S), bf),
        ],
        compiler_params=_params(1),
        name="proj",
    )(x2d, g_pre, w1, g_ckv, w_uk, w_uvt, cos_t, sin_a, sin_b)


def _mla_kernel(q_ref, k_ref, vt_ref, o_ref, sa_ref, sb_ref):
    seq = k_ref.shape[1]
    tq = sa_ref.shape[1]
    n_units = MLA_HEADS * (q_ref.shape[1] // tq)
    chunks = [slice(c * CK_MLA, (c + 1) * CK_MLA) for c in range(seq // CK_MLA)]

    def unit(u):
        hd, qt = u % MLA_HEADS, u // MLA_HEADS
        start = qt * tq if isinstance(u, int) else pl.multiple_of(qt * tq, tq)
        return hd, pl.ds(start, tq)

    def scores(u, s_ref):
        hd, rows = unit(u)
        q = q_ref[hd, rows, :]
        m = None
        for ck in chunks:
            s = lax.dot_general(k_ref[hd, ck, :], q, _NT, preferred_element_type=jnp.float32)
            s_ref[ck, :] = s
            cm = jnp.max(s, axis=0, keepdims=True)
            m = cm if m is None else jnp.maximum(m, cm)
        return m

    def attend(u, s_ref, m):
        hd, rows = unit(u)
        l = jnp.zeros_like(m)
        acc = jnp.zeros((MLA_V, tq), jnp.float32)
        for ck in chunks:
            p = jnp.exp2(s_ref[ck, :] - m)
            l = l + jnp.sum(p, axis=0, keepdims=True)
            acc = acc + jnp.dot(vt_ref[hd, :, ck], p.astype(jnp.bfloat16),
                                preferred_element_type=jnp.float32)
        o_ref[hd, rows, :] = (acc / l).T.astype(o_ref.dtype)

    def body(j, m_a):
        m_b = scores(2 * j + 1, sb_ref)
        attend(2 * j, sa_ref, m_a)
        m_next = scores(2 * j + 2, sa_ref)
        attend(2 * j + 1, sb_ref, m_b)
        return m_next

    assert n_units % 2 == 0
    m_a = lax.fori_loop(0, n_units // 2 - 1, body, scores(0, sa_ref))
    m_b = scores(n_units - 1, sb_ref)
    attend(n_units - 2, sa_ref, m_a)
    attend(n_units - 1, sb_ref, m_b)


def _mla_attention(q, k, vt, B, S):
    T = B * S
    rows = TQ_MLA * QT_MLA
    nq = S // rows
    kv_bytes = S * MLA_HEADS * (QK_PAD + MLA_V) * 2
    mode = dict(pipeline_mode=pl.Buffered(1)) if 2 * kv_bytes > VMEM_LIMIT // 2 else {}
    return pl.pallas_call(
        _mla_kernel,
        grid=(B, nq),
        in_specs=[
            pl.BlockSpec((MLA_HEADS, rows, QK_PAD), lambda b, i: (0, b * nq + i, 0)),
            pl.BlockSpec((MLA_HEADS, S, QK_PAD), lambda b, i: (0, b, 0), **mode),
            pl.BlockSpec((None, MLA_HEADS, MLA_V, S), lambda b, i: (b, 0, 0, 0), **mode),
        ],
        out_specs=pl.BlockSpec((MLA_HEADS, rows, MLA_V), lambda b, i: (0, b * nq + i, 0)),
        out_shape=jax.ShapeDtypeStruct((MLA_HEADS, T, MLA_V), jnp.bfloat16),
        scratch_shapes=[pltpu.VMEM((S, TQ_MLA), jnp.float32), pltpu.VMEM((S, TQ_MLA), jnp.float32)],
        compiler_params=_params(2),
        name="mla",
    )(q, k, vt.reshape(B, MLA_HEADS, MLA_V, S))


def _swa_kernel(sink_ref, bias_ref, q_ref, kp_ref, kc_ref, kn_ref, vp_ref, vc_ref, vn_ref, o_ref,
                *, seq):
    i = pl.program_id(1)
    tq = q_ref.shape[0]
    kcat = jnp.concatenate([kp_ref[...], kc_ref[...], kn_ref[...]], axis=0)
    vcat = jnp.concatenate([vp_ref[...], vc_ref[...], vn_ref[...]], axis=1)
    slot = lax.broadcasted_iota(jnp.int32, (3 * BLOCK, 1), 0)
    for kv in range(SWA_KV_HEADS):
        heads = range(kv * SWA_GROUP, (kv + 1) * SWA_GROUP)
        dsl = slice(kv * SWA_HEAD_DIM, (kv + 1) * SWA_HEAD_DIM)
        sink = jnp.concatenate(
            [jnp.full((1, BLOCK), sink_ref[hd] * LOG2E, jnp.float32) for hd in heads], axis=1)
        bias = bias_ref[kv]
        for jb in range(tq // BLOCK):
            win = slice(jb * BLOCK, (jb + 3) * BLOCK)
            qrows = slice(jb * BLOCK, (jb + 1) * BLOCK)
            key_pos = i * tq + (jb - 1) * BLOCK + slot
            in_seq = (key_pos >= 0) & (key_pos < seq)
            qg = jnp.concatenate(
                [q_ref[qrows, hd * SWA_HEAD_DIM:(hd + 1) * SWA_HEAD_DIM] for hd in heads], axis=0)
            s = lax.dot_general(kcat[win, dsl], qg, _NT, preferred_element_type=jnp.float32)
            s = jnp.where(in_seq, s * (SWA_SCALE * LOG2E) + bias, -jnp.inf)
            m = jnp.maximum(jnp.max(s, axis=0, keepdims=True), sink)
            p = jnp.exp2(s - m)
            denom = jnp.sum(p, axis=0, keepdims=True) + jnp.exp2(sink - m)
            ot = jnp.dot(vcat[dsl, win], p.astype(jnp.bfloat16),
                         preferred_element_type=jnp.float32) / denom
            for g, hd in enumerate(heads):
                o_ref[qrows, hd * SWA_HEAD_DIM:(hd + 1) * SWA_HEAD_DIM] = (
                    ot[:, g * BLOCK:(g + 1) * BLOCK].T.astype(o_ref.dtype))


def _swa_bias():
    r = jnp.arange(3 * BLOCK)[:, None]
    qi = jnp.arange(BLOCK)[None, :]
    dist = jnp.abs(qi + BLOCK - r)
    slopes = 2.0 ** (-8.0 * jnp.arange(1, SWA_HEADS + 1, dtype=jnp.float32) / SWA_HEADS)
    b = jnp.where(dist <= WINDOW, -(slopes * LOG2E)[:, None, None] * dist.astype(jnp.float32),
                  -jnp.inf)
    b = b.reshape(SWA_KV_HEADS, SWA_GROUP, 3 * BLOCK, BLOCK).transpose(0, 2, 1, 3)
    return b.reshape(SWA_KV_HEADS, 3 * BLOCK, SWA_GROUP * BLOCK)


def _swa_attention(sink, qs, ks, vst, B, S):
    T = B * S
    tq = TQ_SWA
    nt = S // tq
    per = tq // BLOCK
    nb = S // BLOCK
    prev = lambda j: jnp.maximum(j * per - 1, 0)
    nxt = lambda j: jnp.minimum((j + 1) * per, nb - 1)
    return pl.pallas_call(
        functools.partial(_swa_kernel, seq=S),
        grid=(B, nt),
        in_specs=[
            pl.BlockSpec(memory_space=pltpu.SMEM),
            _resident((SWA_KV_HEADS, 3 * BLOCK, SWA_GROUP * BLOCK)),
            pl.BlockSpec((tq, N_SWA_Q), lambda b, j: (b * nt + j, 0)),
            pl.BlockSpec((BLOCK, N_SWA_KV), lambda b, j: (b * nb + prev(j), 0)),
            pl.BlockSpec((tq, N_SWA_KV), lambda b, j: (b * nt + j, 0)),
            pl.BlockSpec((BLOCK, N_SWA_KV), lambda b, j: (b * nb + nxt(j), 0)),
            pl.BlockSpec((None, N_SWA_KV, BLOCK), lambda b, j: (b, 0, prev(j))),
            pl.BlockSpec((None, N_SWA_KV, tq), lambda b, j: (b, 0, j)),
            pl.BlockSpec((None, N_SWA_KV, BLOCK), lambda b, j: (b, 0, nxt(j))),
        ],
        out_specs=pl.BlockSpec((tq, SWA_OUT), lambda b, j: (b * nt + j, 0)),
        out_shape=jax.ShapeDtypeStruct((T, SWA_OUT), jnp.bfloat16),
        compiler_params=_params(2),
        name="swa",
    )(sink, _swa_bias(), qs, ks, ks, ks, vst, vst, vst)


def _out_kernel(oa_ref, ob_ref, x_ref, ga_ref, gb_ref, wo_ref, gpost_ref, x1_ref):
    oa = jnp.concatenate([oa_ref[hd] for hd in range(MLA_HEADS)], axis=1)
    na = _rms(oa.astype(jnp.float32), ga_ref[...]).astype(jnp.bfloat16)
    nb = _rms(ob_ref[...].astype(jnp.float32), gb_ref[...]).astype(jnp.bfloat16)
    o = jnp.dot(jnp.concatenate([na, nb], axis=1), wo_ref[...],
                preferred_element_type=jnp.float32)
    x1_ref[...] = x_ref[...] + _rms(o, gpost_ref[...])


def _out_projection(oa, ob, x2d, g_a, g_b, w_o, g_post):
    T = x2d.shape[0]
    tm = TM_OUT
    row = lambda i: (i, 0)
    return pl.pallas_call(
        _out_kernel,
        grid=(T // tm,),
        in_specs=[
            pl.BlockSpec((MLA_HEADS, tm, MLA_V), lambda i: (0, i, 0)),
            pl.BlockSpec((tm, SWA_OUT), row),
            pl.BlockSpec((tm, D_MODEL), row),
            _resident((1, MLA_OUT)),
            _resident((1, SWA_OUT)),
            _resident((MLA_OUT + SWA_OUT, D_MODEL)),
            _resident((1, D_MODEL)),
        ],
        out_specs=pl.BlockSpec((tm, D_MODEL), row),
        out_shape=jax.ShapeDtypeStruct((T, D_MODEL), jnp.float32),
        compiler_params=_params(1),
        name="outproj",
    )(oa, ob, x2d, g_a, g_b, w_o, g_post)


def _ffn_kernel(x1_ref, gpre_ref, wg_ref, wu_ref, wd_ref, gpost_ref, y_ref, h_ref):
    j = pl.program_id(1)
    norm_tiles = [slice(r0, r0 + NORM_ROWS) for r0 in range(0, h_ref.shape[0], NORM_ROWS)]

    @pl.when(j == 0)
    def _():
        for rows in norm_tiles:
            h_ref[rows, :] = _rms(x1_ref[rows, :], gpre_ref[...]).astype(h_ref.dtype)
        y_ref[...] = jnp.zeros_like(y_ref)

    for r0 in range(0, h_ref.shape[0], SUB_FFN):
        rows = slice(r0, r0 + SUB_FFN)
        h = h_ref[rows, :]
        gate = jnp.dot(h, wg_ref[...], preferred_element_type=jnp.float32)
        up = jnp.dot(h, wu_ref[...], preferred_element_type=jnp.float32)
        act = (gate * jax.nn.sigmoid(gate) * up).astype(jnp.bfloat16)
        y_ref[rows, :] += jnp.dot(act, wd_ref[...], preferred_element_type=jnp.float32)

    @pl.when(j == pl.num_programs(1) - 1)
    def _():
        for rows in norm_tiles:
            y_ref[rows, :] = x1_ref[rows, :] + _rms(y_ref[rows, :], gpost_ref[...])


def _ffn(x1, g_pre, w_gate, w_up, w_down, g_post):
    T = x1.shape[0]
    tm, tf = TM_FFN, TF_FFN
    row = lambda i, j: (i, 0)
    return pl.pallas_call(
        _ffn_kernel,
        grid=(T // tm, D_FF // tf),
        in_specs=[
            pl.BlockSpec((tm, D_MODEL), row),
            _resident((1, D_MODEL)),
            pl.BlockSpec((D_MODEL, tf), lambda i, j: (0, j)),
            pl.BlockSpec((D_MODEL, tf), lambda i, j: (0, j)),
            pl.BlockSpec((tf, D_MODEL), lambda i, j: (j, 0)),
            _resident((1, D_MODEL)),
        ],
        out_specs=pl.BlockSpec((tm, D_MODEL), row),
        out_shape=jax.ShapeDtypeStruct((T, D_MODEL), jnp.float32),
        scratch_shapes=[pltpu.VMEM((tm, D_MODEL), jnp.bfloat16)],
        compiler_params=_params(2),
        name="ffn",
    )(x1, g_pre, w_gate, w_up, w_down, g_post)


def _rope_tables(S):
    inv = ROPE_THETA ** (-jnp.arange(0, MLA_ROPE, 2, dtype=jnp.float32) / MLA_ROPE)
    ang = jnp.arange(S, dtype=jnp.float32)[:, None] * inv[None, :]
    c, s = jnp.cos(ang), jnp.sin(ang)
    z = jnp.zeros_like(c)
    cos_t = jnp.concatenate([c, c, c, c], axis=1)
    sin_a = jnp.concatenate([-s, z, -s, z], axis=1)
    sin_b = jnp.concatenate([z, s, z, s], axis=1)
    return cos_t, sin_a, sin_b


def _prep_weights(w_in, w_uk, w_uv, w_o, w_gate, w_up, w_down):
    bf = jnp.bfloat16
    wq = w_in[:, :N_MLA_Q].reshape(D_MODEL, MLA_HEADS, MLA_NOPE + MLA_ROPE)
    w_nope = wq[:, :, :MLA_NOPE].reshape(D_MODEL, MLA_HEADS * MLA_NOPE)
    w_qpe = wq[:, :, MLA_NOPE:].reshape(D_MODEL, MLA_HEADS * MLA_ROPE)
    o = N_MLA_Q
    w_ckv = w_in[:, o:o + KV_RANK]; o += KV_RANK
    w_kpe = jnp.pad(w_in[:, o:o + MLA_ROPE], ((0, 0), (0, LANES - MLA_ROPE))); o += MLA_ROPE
    w_swa = w_in[:, o:]
    w1 = jnp.concatenate([w_nope, w_qpe, w_swa, w_ckv, w_kpe], axis=1).astype(bf)
    assert w1.shape == (D_MODEL, N_PROJ)
    return dict(
        w1=w1,
        w_uk=w_uk.reshape(KV_RANK, MLA_HEADS * MLA_NOPE).astype(bf),
        w_uvt=w_uv.reshape(KV_RANK, MLA_OUT).T.astype(bf),
        w_o=w_o.astype(bf), w_gate=w_gate.astype(bf), w_up=w_up.astype(bf),
        w_down=w_down.astype(bf))


def _layer(x, w, g_pre_attn, g_ckv, sink, g_out_mla, g_out_swa, g_post_attn, g_pre_ffn, g_post_ffn):
    B, S, _ = x.shape
    x2d = x.reshape(B * S, D_MODEL)
    cos_t, sin_a, sin_b = _rope_tables(S)
    q, k, vt, qs, ks, vst = _projection(x2d, B, S, g_pre_attn, w["w1"], g_ckv, w["w_uk"],
                                        w["w_uvt"], cos_t, sin_a, sin_b)
    oa = _mla_attention(q, k, vt, B, S)
    ob = _swa_attention(sink, qs, ks, vst, B, S)
    x1 = _out_projection(oa, ob, x2d, g_out_mla, g_out_swa, w["w_o"], g_post_attn)
    y = _ffn(x1, g_pre_ffn, w["w_gate"], w["w_up"], w["w_down"], g_post_ffn)
    return y.reshape(B, S, D_MODEL)


def kernel(x_prompt, x_sample, g_pre_attn, w_in, g_ckv, w_uk, w_uv, sink, g_out_mla, g_out_swa,
           w_o, g_post_attn, g_pre_ffn, w_gate, w_up, w_down, g_post_ffn):
    assert w_in.shape[0] == 1, "single layer"
    w = _prep_weights(w_in[0], w_uk[0], w_uv[0], w_o[0], w_gate[0], w_up[0], w_down[0])
    gains = (g_pre_attn, g_ckv, sink[0], g_out_mla, g_out_swa, g_post_attn, g_pre_ffn, g_post_ffn)
    return (_layer(x_prompt, w, *gains), _layer(x_sample, w, *gains))
```

```python
import functools
import math

import jax
import jax.numpy as jnp
from jax import lax
from jax.experimental import pallas as pl
from jax.experimental.pallas import tpu as pltpu

D_MODEL = 2048
MLA_HEADS = 8
MLA_NOPE = 128
MLA_ROPE = 64
MLA_V = 128
KV_RANK = 512
ROPE_THETA = 10000.0
SWA_HEADS = 8
SWA_KV_HEADS = 2
SWA_HEAD_DIM = 128
SWA_GROUP = SWA_HEADS // SWA_KV_HEADS
WINDOW = 128
BLOCK = 128
D_FF = -(-8 * D_MODEL // (3 * 256)) * 256
EPS = 1e-6

N_MLA_Q = MLA_HEADS * (MLA_NOPE + MLA_ROPE)
N_SWA_Q = SWA_HEADS * SWA_HEAD_DIM
N_SWA_KV = SWA_KV_HEADS * SWA_HEAD_DIM
MLA_OUT = MLA_HEADS * MLA_V
SWA_OUT = SWA_HEADS * SWA_HEAD_DIM

LANES = 128
QK_PAD = 2 * LANES
OFF_QPE = MLA_HEADS * MLA_NOPE
OFF_QS = OFF_QPE + MLA_HEADS * MLA_ROPE
OFF_KS = OFF_QS + N_SWA_Q
OFF_VS = OFF_KS + N_SWA_KV
OFF_CKV = OFF_VS + N_SWA_KV
OFF_KPE = OFF_CKV + KV_RANK
N_PROJ = OFF_KPE + LANES

LOG2E = math.log2(math.e)
MLA_SCALE = (MLA_NOPE + MLA_ROPE) ** -0.5
SWA_SCALE = SWA_HEAD_DIM ** -0.5

VMEM_LIMIT = 56 * 1024 * 1024

TM_PROJ = 512
TQ_MLA = 256
QT_MLA = 2
V_ROWS = MLA_V + 16
CK_MLA = 512
TQ_SWA = 512
TM_OUT = 512
TM_FFN = 1024
TF_FFN = 512
SUB_FFN = 256

_NT = (((1,), (1,)), ((), ()))


def _params(n_axes):
    return pltpu.CompilerParams(
        dimension_semantics=("arbitrary",) * n_axes, vmem_limit_bytes=VMEM_LIMIT)


def _resident(shape):
    return pl.BlockSpec(shape, lambda *_: (0,) * len(shape), pipeline_mode=pl.Buffered(1))


def _rms(x, g):
    var = jnp.mean(x * x, axis=-1, keepdims=True)
    return x * lax.rsqrt(var + EPS) * g


def _rope_pairs(x, cos_t, sin_a, sin_b):
    return x * cos_t + pltpu.roll(x, 96, 1) * sin_a + pltpu.roll(x, 32, 1) * sin_b


def _proj_kernel(x_ref, g_ref, w1_ref, gckv_ref, wuk_ref, wuvt_ref, cos_ref, sina_ref, sinb_ref,
                 q_ref, k_ref, vt_ref, qs_ref, ks_ref, vst_ref):
    h = _rms(x_ref[...], g_ref[...]).astype(jnp.bfloat16)
    cos_t, sin_a, sin_b = cos_ref[...], sina_ref[...], sinb_ref[...]
    bf = jnp.bfloat16

    def proj(lo, hi):
        return jnp.dot(h, w1_ref[:, lo:hi], preferred_element_type=jnp.float32)

    low_half = lax.broadcasted_iota(jnp.int32, cos_t.shape, 1) < MLA_ROPE
    q_scale = MLA_SCALE * LOG2E
    q_pe = proj(OFF_QPE, OFF_QS) * q_scale
    for pair in range(MLA_HEADS // 2):
        nope = proj(pair * 2 * MLA_NOPE, (pair + 1) * 2 * MLA_NOPE) * q_scale
        rot = _rope_pairs(q_pe[:, pair * LANES:(pair + 1) * LANES], cos_t, sin_a, sin_b)
        q_ref[2 * pair, :, :LANES] = nope[:, :MLA_NOPE].astype(bf)
        q_ref[2 * pair, :, LANES:] = jnp.where(low_half, rot, 0.0).astype(bf)
        q_ref[2 * pair + 1, :, :LANES] = nope[:, MLA_NOPE:].astype(bf)
        q_ref[2 * pair + 1, :, LANES:] = jnp.where(low_half, pltpu.roll(rot, 64, 1), 0.0).astype(bf)

    qs_ref[...] = (proj(OFF_QS, OFF_KS) * (SWA_SCALE * LOG2E)).astype(bf)
    kvs = proj(OFF_KS, OFF_CKV)
    ks_ref[...] = kvs[:, :N_SWA_KV].astype(bf)
    vst_ref[...] = kvs[:, N_SWA_KV:].T.astype(bf)

    lat = proj(OFF_CKV, N_PROJ)
    c_kv = _rms(lat[:, :KV_RANK], gckv_ref[...]).astype(bf)
    kpe = _rope_pairs(lat[:, KV_RANK:], cos_t, sin_a, sin_b).astype(bf)
    k_nope = jnp.dot(c_kv, wuk_ref[...], preferred_element_type=jnp.float32)
    for hd in range(MLA_HEADS):
        k_ref[hd, :, :LANES] = k_nope[:, hd * MLA_NOPE:(hd + 1) * MLA_NOPE].astype(bf)
        k_ref[hd, :, LANES:] = kpe
    vt = lax.dot_general(wuvt_ref[...], c_kv, _NT, preferred_element_type=jnp.float32)
    for hd in range(MLA_HEADS):
        vt_ref[hd, :MLA_V, :] = vt[hd * MLA_V:(hd + 1) * MLA_V, :].astype(bf)
        vt_ref[hd, MLA_V:, :] = jnp.ones((V_ROWS - MLA_V, vt.shape[1]), bf)


def _projection(x2d, B, S, g_pre, w1, g_ckv, w_uk, w_uvt, cos_t, sin_a, sin_b):
    T = B * S
    tm = TM_PROJ
    spt = S // tm
    row = lambda i: (i, 0)
    tab = pl.BlockSpec((tm, LANES), lambda i: (i % spt, 0))
    tposed = lambda n: pl.BlockSpec((None, n, tm), lambda i: (i // spt, 0, i % spt))
    bf = jnp.bfloat16
    return pl.pallas_call(
        _proj_kernel,
        grid=(T // tm,),
        in_specs=[
            pl.BlockSpec((tm, D_MODEL), row),
            _resident((1, D_MODEL)),
            _resident((D_MODEL, N_PROJ)),
            _resident((1, KV_RANK)),
            _resident((KV_RANK, MLA_HEADS * MLA_NOPE)),
            _resident((MLA_OUT, KV_RANK)),
            tab, tab, tab,
        ],
        out_specs=[
            pl.BlockSpec((MLA_HEADS, tm, QK_PAD), lambda i: (0, i, 0)),
            pl.BlockSpec((MLA_HEADS, tm, QK_PAD), lambda i: (0, i, 0)),
            pl.BlockSpec((None, MLA_HEADS, V_ROWS, tm), lambda i: (i // spt, 0, 0, i % spt)),
            pl.BlockSpec((tm, N_SWA_Q), row),
            pl.BlockSpec((tm, N_SWA_KV), row),
            tposed(N_SWA_KV),
        ],
        out_shape=[
            jax.ShapeDtypeStruct((MLA_HEADS, T, QK_PAD), bf),
            jax.ShapeDtypeStruct((MLA_HEADS, T, QK_PAD), bf),
            jax.ShapeDtypeStruct((B, MLA_HEADS, V_ROWS, S), bf),
            jax.ShapeDtypeStruct((T, N_SWA_Q), bf),
            jax.ShapeDtypeStruct((T, N_SWA_KV), bf),
            jax.ShapeDtypeStruct((B, N_SWA_KV, S), bf),
        ],
        compiler_params=_params(1),
        name="proj",
    )(x2d, g_pre, w1, g_ckv, w_uk, w_uvt, cos_t, sin_a, sin_b)


def _mla_kernel(q_ref, k_ref, vt_ref, o_ref, sa_ref, sb_ref):
    seq = k_ref.shape[1]
    tq = sa_ref.shape[1]
    n_units = MLA_HEADS * (q_ref.shape[1] // tq)
    chunks = [slice(c * CK_MLA, (c + 1) * CK_MLA) for c in range(seq // CK_MLA)]

    def unit(u):
        hd, qt = u % MLA_HEADS, u // MLA_HEADS
        return hd, slice(qt * tq, (qt + 1) * tq)

    def scores(u, s_ref):
        hd, rows = unit(u)
        q = q_ref[hd, rows, :]
        m = None
        for ck in chunks:
            s = lax.dot_general(k_ref[hd, ck, :], q, _NT, preferred_element_type=jnp.float32)
            s_ref[ck, :] = s
            cm = jnp.max(s, axis=0, keepdims=True)
            m = cm if m is None else jnp.maximum(m, cm)
        return m

    def attend(u, s_ref, m):
        hd, rows = unit(u)
        acc = jnp.zeros((V_ROWS, tq), jnp.float32)
        for ck in chunks:
            p = jnp.exp2(s_ref[ck, :] - m)
            acc = acc + jnp.dot(vt_ref[hd, :, ck], p.astype(jnp.bfloat16),
                                preferred_element_type=jnp.float32)
        o_ref[hd, rows, :] = (acc[:MLA_V] / acc[MLA_V:MLA_V + 1]).T.astype(o_ref.dtype)

    bufs = (sa_ref, sb_ref)
    m_prev = scores(0, bufs[0])
    for u in range(1, n_units):
        m_cur = scores(u, bufs[u % 2])
        attend(u - 1, bufs[(u - 1) % 2], m_prev)
        m_prev = m_cur
    attend(n_units - 1, bufs[(n_units - 1) % 2], m_prev)


def _mla_attention(q, k, vt, B, S):
    T = B * S
    rows = TQ_MLA * QT_MLA
    nq = S // rows
    kv_bytes = S * MLA_HEADS * (QK_PAD + V_ROWS) * 2
    mode = dict(pipeline_mode=pl.Buffered(1)) if 2 * kv_bytes > VMEM_LIMIT // 2 else {}
    return pl.pallas_call(
        _mla_kernel,
        grid=(B, nq),
        in_specs=[
            pl.BlockSpec((MLA_HEADS, rows, QK_PAD), lambda b, i: (0, b * nq + i, 0)),
            pl.BlockSpec((MLA_HEADS, S, QK_PAD), lambda b, i: (0, b, 0), **mode),
            pl.BlockSpec((None, MLA_HEADS, V_ROWS, S), lambda b, i: (b, 0, 0, 0), **mode),
        ],
        out_specs=pl.BlockSpec((MLA_HEADS, rows, MLA_V), lambda b, i: (0, b * nq + i, 0)),
        out_shape=jax.ShapeDtypeStruct((MLA_HEADS, T, MLA_V), jnp.bfloat16),
        scratch_shapes=[pltpu.VMEM((S, TQ_MLA), jnp.float32), pltpu.VMEM((S, TQ_MLA), jnp.float32)],
        compiler_params=_params(2),
        name="mla",
    )(q, k, vt)


def _swa_kernel(sink_ref, bias_ref, q_ref, kp_ref, kc_ref, kn_ref, vp_ref, vc_ref, vn_ref, o_ref,
                *, seq):
    i = pl.program_id(1)
    tq = q_ref.shape[0]
    kcat = jnp.concatenate([kp_ref[...], kc_ref[...], kn_ref[...]], axis=0)
    vcat = jnp.concatenate([vp_ref[...], vc_ref[...], vn_ref[...]], axis=1)
    slot = lax.broadcasted_iota(jnp.int32, (3 * BLOCK, 1), 0)
    for kv in range(SWA_KV_HEADS):
        heads = range(kv * SWA_GROUP, (kv + 1) * SWA_GROUP)
        dsl = slice(kv * SWA_HEAD_DIM, (kv + 1) * SWA_HEAD_DIM)
        sink = jnp.concatenate(
            [jnp.full((1, BLOCK), sink_ref[hd] * LOG2E, jnp.float32) for hd in heads], axis=1)
        bias = bias_ref[kv]
        for jb in range(tq // BLOCK):
            win = slice(jb * BLOCK, (jb + 3) * BLOCK)
            qrows = slice(jb * BLOCK, (jb + 1) * BLOCK)
            qg = jnp.concatenate(
                [q_ref[qrows, hd * SWA_HEAD_DIM:(hd + 1) * SWA_HEAD_DIM] for hd in heads], axis=0)
            s = lax.dot_general(kcat[win, dsl], qg, _NT, preferred_element_type=jnp.float32) + bias
            if jb == 0 or jb == tq // BLOCK - 1:
                key_pos = i * tq + (jb - 1) * BLOCK + slot
                s = jnp.where((key_pos >= 0) & (key_pos < seq), s, -jnp.inf)
            m = jnp.maximum(jnp.max(s, axis=0, keepdims=True), sink)
            p = jnp.exp2(s - m)
            denom = jnp.sum(p, axis=0, keepdims=True) + jnp.exp2(sink - m)
            ot = jnp.dot(vcat[dsl, win], p.astype(jnp.bfloat16),
                         preferred_element_type=jnp.float32) / denom
            for g, hd in enumerate(heads):
                o_ref[qrows, hd * SWA_HEAD_DIM:(hd + 1) * SWA_HEAD_DIM] = (
                    ot[:, g * BLOCK:(g + 1) * BLOCK].T.astype(o_ref.dtype))


def _swa_bias():
    r = jnp.arange(3 * BLOCK)[:, None]
    qi = jnp.arange(BLOCK)[None, :]
    dist = jnp.abs(qi + BLOCK - r)
    slopes = 2.0 ** (-8.0 * jnp.arange(1, SWA_HEADS + 1, dtype=jnp.float32) / SWA_HEADS)
    b = jnp.where(dist <= WINDOW, -(slopes * LOG2E)[:, None, None] * dist.astype(jnp.float32),
                  -jnp.inf)
    b = b.reshape(SWA_KV_HEADS, SWA_GROUP, 3 * BLOCK, BLOCK).transpose(0, 2, 1, 3)
    return b.reshape(SWA_KV_HEADS, 3 * BLOCK, SWA_GROUP * BLOCK)


def _swa_attention(sink, qs, ks, vst, B, S):
    T = B * S
    tq = TQ_SWA
    nt = S // tq
    per = tq // BLOCK
    nb = S // BLOCK
    prev = lambda j: jnp.maximum(j * per - 1, 0)
    nxt = lambda j: jnp.minimum((j + 1) * per, nb - 1)
    return pl.pallas_call(
        functools.partial(_swa_kernel, seq=S),
        grid=(B, nt),
        in_specs=[
            pl.BlockSpec(memory_space=pltpu.SMEM),
            _resident((SWA_KV_HEADS, 3 * BLOCK, SWA_GROUP * BLOCK)),
            pl.BlockSpec((tq, N_SWA_Q), lambda b, j: (b * nt + j, 0)),
            pl.BlockSpec((BLOCK, N_SWA_KV), lambda b, j: (b * nb + prev(j), 0)),
            pl.BlockSpec((tq, N_SWA_KV), lambda b, j: (b * nt + j, 0)),
            pl.BlockSpec((BLOCK, N_SWA_KV), lambda b, j: (b * nb + nxt(j), 0)),
            pl.BlockSpec((None, N_SWA_KV, BLOCK), lambda b, j: (b, 0, prev(j))),
            pl.BlockSpec((None, N_SWA_KV, tq), lambda b, j: (b, 0, j)),
            pl.BlockSpec((None, N_SWA_KV, BLOCK), lambda b, j: (b, 0, nxt(j))),
        ],
        out_specs=pl.BlockSpec((tq, SWA_OUT), lambda b, j: (b * nt + j, 0)),
        out_shape=jax.ShapeDtypeStruct((T, SWA_OUT), jnp.bfloat16),
        compiler_params=_params(2),
        name="swa",
    )(sink, _swa_bias(), qs, ks, ks, ks, vst, vst, vst)


def _out_kernel(oa_ref, ob_ref, x_ref, ga_ref, gb_ref, wo_ref, gpost_ref, x1_ref):
    oa = jnp.concatenate([oa_ref[hd] for hd in range(MLA_HEADS)], axis=1)
    na = _rms(oa.astype(jnp.float32), ga_ref[...]).astype(jnp.bfloat16)
    nb = _rms(ob_ref[...].astype(jnp.float32), gb_ref[...]).astype(jnp.bfloat16)
    o = jnp.dot(jnp.concatenate([na, nb], axis=1), wo_ref[...],
                preferred_element_type=jnp.float32)
    x1_ref[...] = x_ref[...] + _rms(o, gpost_ref[...])


def _out_projection(oa, ob, x2d, g_a, g_b, w_o, g_post):
    T = x2d.shape[0]
    tm = TM_OUT
    row = lambda i: (i, 0)
    return pl.pallas_call(
        _out_kernel,
        grid=(T // tm,),
        in_specs=[
            pl.BlockSpec((MLA_HEADS, tm, MLA_V), lambda i: (0, i, 0)),
            pl.BlockSpec((tm, SWA_OUT), row),
            pl.BlockSpec((tm, D_MODEL), row),
            _resident((1, MLA_OUT)),
            _resident((1, SWA_OUT)),
            _resident((MLA_OUT + SWA_OUT, D_MODEL)),
            _resident((1, D_MODEL)),
        ],
        out_specs=pl.BlockSpec((tm, D_MODEL), row),
        out_shape=jax.ShapeDtypeStruct((T, D_MODEL), jnp.float32),
        compiler_params=_params(1),
        name="outproj",
    )(oa, ob, x2d, g_a, g_b, w_o, g_post)


def _ffn_kernel(x1_ref, gpre_ref, wg_ref, wu_ref, wd_ref, gpost_ref, y_ref, h_ref):
    j = pl.program_id(1)
    last_j = pl.num_programs(1) - 1

    def step(first, last):
        for r0 in range(0, h_ref.shape[0], SUB_FFN):
            rows = slice(r0, r0 + SUB_FFN)
            if first:
                h = _rms(x1_ref[rows, :], gpre_ref[...]).astype(h_ref.dtype)
                h_ref[rows, :] = h
            else:
                h = h_ref[rows, :]
            gate = jnp.dot(h, wg_ref[...], preferred_element_type=jnp.float32)
            up = jnp.dot(h, wu_ref[...], preferred_element_type=jnp.float32)
            act = (gate * jax.nn.sigmoid(gate) * up).astype(jnp.bfloat16)
            acc = jnp.dot(act, wd_ref[...], preferred_element_type=jnp.float32)
            if not first:
                acc = y_ref[rows, :] + acc
            if last:
                acc = x1_ref[rows, :] + _rms(acc, gpost_ref[...])
            y_ref[rows, :] = acc

    pl.when(j == 0)(functools.partial(step, True, False))
    pl.when((j > 0) & (j < last_j))(functools.partial(step, False, False))
    pl.when(j == last_j)(functools.partial(step, False, True))


def _ffn(x1, g_pre, w_gate, w_up, w_down, g_post):
    T = x1.shape[0]
    tm, tf = TM_FFN, TF_FFN
    assert D_FF // tf >= 2, "first and last d_ff chunk must be different steps"
    row = lambda i, j: (i, 0)
    return pl.pallas_call(
        _ffn_kernel,
        grid=(T // tm, D_FF // tf),
        in_specs=[
            pl.BlockSpec((tm, D_MODEL), row),
            _resident((1, D_MODEL)),
            pl.BlockSpec((D_MODEL, tf), lambda i, j: (0, j)),
            pl.BlockSpec((D_MODEL, tf), lambda i, j: (0, j)),
            pl.BlockSpec((tf, D_MODEL), lambda i, j: (j, 0)),
            _resident((1, D_MODEL)),
        ],
        out_specs=pl.BlockSpec((tm, D_MODEL), row),
        out_shape=jax.ShapeDtypeStruct((T, D_MODEL), jnp.float32),
        scratch_shapes=[pltpu.VMEM((tm, D_MODEL), jnp.bfloat16)],
        compiler_params=_params(2),
        name="ffn",
    )(x1, g_pre, w_gate, w_up, w_down, g_post)


def _rope_tables(S):
    inv = ROPE_THETA ** (-jnp.arange(0, MLA_ROPE, 2, dtype=jnp.float32) / MLA_ROPE)
    ang = jnp.arange(S, dtype=jnp.float32)[:, None] * inv[None, :]
    c, s = jnp.cos(ang), jnp.sin(ang)
    z = jnp.zeros_like(c)
    cos_t = jnp.concatenate([c, c, c, c], axis=1)
    sin_a = jnp.concatenate([-s, z, -s, z], axis=1)
    sin_b = jnp.concatenate([z, s, z, s], axis=1)
    return cos_t, sin_a, sin_b


def _prep_weights(w_in, w_uk, w_uv, w_o, w_gate, w_up, w_down):
    bf = jnp.bfloat16
    wq = w_in[:, :N_MLA_Q].reshape(D_MODEL, MLA_HEADS, MLA_NOPE + MLA_ROPE)
    w_nope = wq[:, :, :MLA_NOPE].reshape(D_MODEL, MLA_HEADS * MLA_NOPE)
    w_qpe = wq[:, :, MLA_NOPE:].reshape(D_MODEL, MLA_HEADS * MLA_ROPE)
    o = N_MLA_Q
    w_ckv = w_in[:, o:o + KV_RANK]; o += KV_RANK
    w_kpe = jnp.pad(w_in[:, o:o + MLA_ROPE], ((0, 0), (0, LANES - MLA_ROPE))); o += MLA_ROPE
    w_swa = w_in[:, o:]
    w1 = jnp.concatenate([w_nope, w_qpe, w_swa, w_ckv, w_kpe], axis=1).astype(bf)
    assert w1.shape == (D_MODEL, N_PROJ)
    return dict(
        w1=w1,
        w_uk=w_uk.reshape(KV_RANK, MLA_HEADS * MLA_NOPE).astype(bf),
        w_uvt=w_uv.reshape(KV_RANK, MLA_OUT).T.astype(bf),
        w_o=w_o.astype(bf), w_gate=w_gate.astype(bf), w_up=w_up.astype(bf),
        w_down=w_down.astype(bf))


def _layer(x, w, g_pre_attn, g_ckv, sink, g_out_mla, g_out_swa, g_post_attn, g_pre_ffn, g_post_ffn):
    B, S, _ = x.shape
    x2d = x.reshape(B * S, D_MODEL)
    cos_t, sin_a, sin_b = _rope_tables(S)
    q, k, vt, qs, ks, vst = _projection(x2d, B, S, g_pre_attn, w["w1"], g_ckv, w["w_uk"],
                                        w["w_uvt"], cos_t, sin_a, sin_b)
    oa = _mla_attention(q, k, vt, B, S)
    ob = _swa_attention(sink, qs, ks, vst, B, S)
    x1 = _out_projection(oa, ob, x2d, g_out_mla, g_out_swa, w["w_o"], g_post_attn)
    y = _ffn(x1, g_pre_ffn, w["w_gate"], w["w_up"], w["w_down"], g_post_ffn)
    return y.reshape(B, S, D_MODEL)


def kernel(x_prompt, x_sample, g_pre_attn, w_in, g_ckv, w_uk, w_uv, sink, g_out_mla, g_out_swa,
           w_o, g_post_attn, g_pre_ffn, w_gate, w_up, w_down, g_post_ffn):
    assert w_in.shape[0] == 1, "single layer"
    w = _prep_weights(w_in[0], w_uk[0], w_uv[0], w_o[0], w_gate[0], w_up[0], w_down[0])
    gains = (g_pre_attn, g_ckv, sink[0], g_out_mla, g_out_swa, g_post_attn, g_pre_ffn, g_post_ffn)
    return (_layer(x_prompt, w, *gains), _layer(x_sample, w, *gains))
```

```python
import functools
import math

import jax
import jax.numpy as jnp
from jax import lax
from jax.experimental import pallas as pl
from jax.experimental.pallas import tpu as pltpu

D_MODEL = 2048
MLA_HEADS = 8
MLA_NOPE = 128
MLA_ROPE = 64
MLA_V = 128
KV_RANK = 512
ROPE_THETA = 10000.0
SWA_HEADS = 8
SWA_KV_HEADS = 2
SWA_HEAD_DIM = 128
SWA_GROUP = SWA_HEADS // SWA_KV_HEADS
WINDOW = 128
BLOCK = 128
D_FF = -(-8 * D_MODEL // (3 * 256)) * 256
EPS = 1e-6

N_MLA_Q = MLA_HEADS * (MLA_NOPE + MLA_ROPE)
N_SWA_Q = SWA_HEADS * SWA_HEAD_DIM
N_SWA_KV = SWA_KV_HEADS * SWA_HEAD_DIM
MLA_OUT = MLA_HEADS * MLA_V
SWA_OUT = SWA_HEADS * SWA_HEAD_DIM

LANES = 128
QK_PAD = 2 * LANES
OFF_QPE = MLA_HEADS * MLA_NOPE
OFF_QS = OFF_QPE + MLA_HEADS * MLA_ROPE
OFF_KS = OFF_QS + N_SWA_Q
OFF_VS = OFF_KS + N_SWA_KV
OFF_CKV = OFF_VS + N_SWA_KV
OFF_KPE = OFF_CKV + KV_RANK
N_PROJ = OFF_KPE + LANES

LOG2E = math.log2(math.e)
MLA_SCALE = (MLA_NOPE + MLA_ROPE) ** -0.5
SWA_SCALE = SWA_HEAD_DIM ** -0.5

VMEM_LIMIT = 56 * 1024 * 1024

TM_PROJ = 512
TQ_MLA = 256
QT_MLA = 2
ONES_ROWS = 16
V_ROWS = MLA_V + ONES_ROWS
CK_MLA = 512
TQ_SWA = 512
TM_OUT = 512
TM_FFN = 1024
TF_FFN = 512
SUB_FFN = 512

_NT = (((1,), (1,)), ((), ()))


def _params(n_axes):
    return pltpu.CompilerParams(
        dimension_semantics=("arbitrary",) * n_axes, vmem_limit_bytes=VMEM_LIMIT)


def _resident(shape):
    return pl.BlockSpec(shape, lambda *_: (0,) * len(shape), pipeline_mode=pl.Buffered(1))


def _rms(x, g):
    var = jnp.mean(x * x, axis=-1, keepdims=True)
    return x * lax.rsqrt(var + EPS) * g


def _rope_pairs(x, cos_t, sin_a, sin_b):
    return x * cos_t + pltpu.roll(x, 96, 1) * sin_a + pltpu.roll(x, 32, 1) * sin_b


def _proj_kernel(x_ref, g_ref, w1_ref, gckv_ref, wuk_ref, wuvt_ref, cos_ref, sina_ref, sinb_ref,
                 q_ref, k_ref, vt_ref, qs_ref, ks_ref, vst_ref):
    h = _rms(x_ref[...], g_ref[...]).astype(jnp.bfloat16)
    cos_t, sin_a, sin_b = cos_ref[...], sina_ref[...], sinb_ref[...]
    bf = jnp.bfloat16

    def proj(lo, hi):
        return jnp.dot(h, w1_ref[:, lo:hi], preferred_element_type=jnp.float32)

    low_half = lax.broadcasted_iota(jnp.int32, cos_t.shape, 1) < MLA_ROPE
    q_scale = MLA_SCALE * LOG2E
    q_pe = proj(OFF_QPE, OFF_QS) * q_scale
    for pair in range(MLA_HEADS // 2):
        nope = proj(pair * 2 * MLA_NOPE, (pair + 1) * 2 * MLA_NOPE) * q_scale
        rot = _rope_pairs(q_pe[:, pair * LANES:(pair + 1) * LANES], cos_t, sin_a, sin_b)
        q_ref[2 * pair, :, :LANES] = nope[:, :MLA_NOPE].astype(bf)
        q_ref[2 * pair, :, LANES:] = jnp.where(low_half, rot, 0.0).astype(bf)
        q_ref[2 * pair + 1, :, :LANES] = nope[:, MLA_NOPE:].astype(bf)
        q_ref[2 * pair + 1, :, LANES:] = jnp.where(low_half, pltpu.roll(rot, 64, 1), 0.0).astype(bf)

    qs_ref[...] = (proj(OFF_QS, OFF_KS) * (SWA_SCALE * LOG2E)).astype(bf)
    kvs = proj(OFF_KS, OFF_CKV)
    ks_ref[...] = kvs[:, :N_SWA_KV].astype(bf)
    vst_ref[...] = kvs[:, N_SWA_KV:].T.astype(bf)

    lat = proj(OFF_CKV, N_PROJ)
    c_kv = _rms(lat[:, :KV_RANK], gckv_ref[...]).astype(bf)
    kpe = _rope_pairs(lat[:, KV_RANK:], cos_t, sin_a, sin_b).astype(bf)
    k_nope = jnp.dot(c_kv, wuk_ref[...], preferred_element_type=jnp.float32)
    for hd in range(MLA_HEADS):
        k_ref[hd, :, :LANES] = k_nope[:, hd * MLA_NOPE:(hd + 1) * MLA_NOPE].astype(bf)
        k_ref[hd, :, LANES:] = kpe
    vt = lax.dot_general(wuvt_ref[...], c_kv, _NT, preferred_element_type=jnp.float32)
    for hd in range(MLA_HEADS):
        vt_ref[hd, :MLA_V, :] = vt[hd * MLA_V:(hd + 1) * MLA_V, :].astype(bf)
        vt_ref[hd, MLA_V:, :] = jnp.ones((ONES_ROWS, vt.shape[1]), bf)


def _projection(x2d, B, S, g_pre, w1, g_ckv, w_uk, w_uvt, cos_t, sin_a, sin_b):
    T = B * S
    tm = TM_PROJ
    spt = S // tm
    row = lambda i: (i, 0)
    tab = pl.BlockSpec((tm, LANES), lambda i: (i % spt, 0))
    tposed = lambda n: pl.BlockSpec((None, n, tm), lambda i: (i // spt, 0, i % spt))
    bf = jnp.bfloat16
    return pl.pallas_call(
        _proj_kernel,
        grid=(T // tm,),
        in_specs=[
            pl.BlockSpec((tm, D_MODEL), row),
            _resident((1, D_MODEL)),
            _resident((D_MODEL, N_PROJ)),
            _resident((1, KV_RANK)),
            _resident((KV_RANK, MLA_HEADS * MLA_NOPE)),
            _resident((MLA_OUT, KV_RANK)),
            tab, tab, tab,
        ],
        out_specs=[
            pl.BlockSpec((MLA_HEADS, tm, QK_PAD), lambda i: (0, i, 0)),
            pl.BlockSpec((MLA_HEADS, tm, QK_PAD), lambda i: (0, i, 0)),
            pl.BlockSpec((None, MLA_HEADS, V_ROWS, tm), lambda i: (i // spt, 0, 0, i % spt)),
            pl.BlockSpec((tm, N_SWA_Q), row),
            pl.BlockSpec((tm, N_SWA_KV), row),
            tposed(N_SWA_KV),
        ],
        out_shape=[
            jax.ShapeDtypeStruct((MLA_HEADS, T, QK_PAD), bf),
            jax.ShapeDtypeStruct((MLA_HEADS, T, QK_PAD), bf),
            jax.ShapeDtypeStruct((B, MLA_HEADS, V_ROWS, S), bf),
            jax.ShapeDtypeStruct((T, N_SWA_Q), bf),
            jax.ShapeDtypeStruct((T, N_SWA_KV), bf),
            jax.ShapeDtypeStruct((B, N_SWA_KV, S), bf),
        ],
        compiler_params=_params(1),
        name="proj",
    )(x2d, g_pre, w1, g_ckv, w_uk, w_uvt, cos_t, sin_a, sin_b)


def _mla_kernel(q_ref, k_ref, vt_ref, o_ref, sa_ref, sb_ref):
    seq = k_ref.shape[1]
    tq = sa_ref.shape[1]
    n_units = MLA_HEADS * (q_ref.shape[1] // tq)
    chunks = [slice(c * CK_MLA, (c + 1) * CK_MLA) for c in range(seq // CK_MLA)]

    def unit(u):
        hd, qt = u % MLA_HEADS, u // MLA_HEADS
        return hd, slice(qt * tq, (qt + 1) * tq)

    def scores(u, s_ref):
        hd, rows = unit(u)
        q = q_ref[hd, rows, :]
        m = None
        for ck in chunks:
            s = lax.dot_general(k_ref[hd, ck, :], q, _NT, preferred_element_type=jnp.float32)
            s_ref[ck, :] = s
            cm = jnp.max(s, axis=0, keepdims=True)
            m = cm if m is None else jnp.maximum(m, cm)
        return m

    def attend(u, s_ref, m):
        hd, rows = unit(u)
        acc = jnp.zeros((V_ROWS, tq), jnp.float32)
        for ck in chunks:
            p = jnp.exp2(s_ref[ck, :] - m)
            acc = acc + jnp.dot(vt_ref[hd, :, ck], p.astype(jnp.bfloat16),
                                preferred_element_type=jnp.float32)
        o_ref[hd, rows, :] = (acc[:MLA_V] / acc[MLA_V:MLA_V + 1]).T.astype(o_ref.dtype)

    bufs = (sa_ref, sb_ref)
    m_prev = scores(0, bufs[0])
    for u in range(1, n_units):
        m_cur = scores(u, bufs[u % 2])
        attend(u - 1, bufs[(u - 1) % 2], m_prev)
        m_prev = m_cur
    attend(n_units - 1, bufs[(n_units - 1) % 2], m_prev)


def _mla_attention(q, k, vt, B, S):
    T = B * S
    rows = TQ_MLA * QT_MLA
    nq = S // rows
    kv_bytes = S * MLA_HEADS * (QK_PAD + V_ROWS) * 2
    mode = dict(pipeline_mode=pl.Buffered(1)) if 2 * kv_bytes > VMEM_LIMIT // 2 else {}
    return pl.pallas_call(
        _mla_kernel,
        grid=(B, nq),
        in_specs=[
            pl.BlockSpec((MLA_HEADS, rows, QK_PAD), lambda b, i: (0, b * nq + i, 0)),
            pl.BlockSpec((MLA_HEADS, S, QK_PAD), lambda b, i: (0, b, 0), **mode),
            pl.BlockSpec((None, MLA_HEADS, V_ROWS, S), lambda b, i: (b, 0, 0, 0), **mode),
        ],
        out_specs=pl.BlockSpec((MLA_HEADS, rows, MLA_V), lambda b, i: (0, b * nq + i, 0)),
        out_shape=jax.ShapeDtypeStruct((MLA_HEADS, T, MLA_V), jnp.bfloat16),
        scratch_shapes=[pltpu.VMEM((S, TQ_MLA), jnp.float32), pltpu.VMEM((S, TQ_MLA), jnp.float32)],
        compiler_params=_params(2),
        name="mla",
    )(q, k, vt)


def _swa_kernel(sink_ref, bias_ref, q_ref, kp_ref, kc_ref, kn_ref, vp_ref, vc_ref, vn_ref, o_ref,
                sa_ref, sb_ref, *, seq):
    i = pl.program_id(1)
    tq = q_ref.shape[0]
    nblk = tq // BLOCK
    kcat = jnp.concatenate([kp_ref[...], kc_ref[...], kn_ref[...]], axis=0)
    vcat = jnp.concatenate([vp_ref[...], vc_ref[...], vn_ref[...]], axis=1)
    slot = lax.broadcasted_iota(jnp.int32, (3 * BLOCK, 1), 0)
    ones = jnp.ones((ONES_ROWS, 3 * BLOCK), jnp.bfloat16)
    sinks = [jnp.concatenate(
        [jnp.full((1, BLOCK), sink_ref[kv * SWA_GROUP + g] * LOG2E, jnp.float32)
         for g in range(SWA_GROUP)], axis=1) for kv in range(SWA_KV_HEADS)]

    def unit(u):
        kv, jb = u // nblk, u % nblk
        dsl = slice(kv * SWA_HEAD_DIM, (kv + 1) * SWA_HEAD_DIM)
        win = slice(jb * BLOCK, (jb + 3) * BLOCK)
        return kv, jb, dsl, win, slice(jb * BLOCK, (jb + 1) * BLOCK)

    def scores(u, s_ref):
        kv, jb, dsl, win, qrows = unit(u)
        qg = jnp.concatenate(
            [q_ref[qrows, (kv * SWA_GROUP + g) * SWA_HEAD_DIM:(kv * SWA_GROUP + g + 1) * SWA_HEAD_DIM]
             for g in range(SWA_GROUP)], axis=0)
        s = lax.dot_general(kcat[win, dsl], qg, _NT, preferred_element_type=jnp.float32)
        s = s + bias_ref[kv]
        if jb == 0 or jb == nblk - 1:
            key_pos = i * tq + (jb - 1) * BLOCK + slot
            s = jnp.where((key_pos >= 0) & (key_pos < seq), s, -jnp.inf)
        s_ref[...] = s
        return jnp.maximum(jnp.max(s, axis=0, keepdims=True), sinks[kv])

    def attend(u, s_ref, m):
        kv, jb, dsl, win, qrows = unit(u)
        p = jnp.exp2(s_ref[...] - m).astype(jnp.bfloat16)
        acc = jnp.dot(jnp.concatenate([vcat[dsl, win], ones], axis=0), p,
                      preferred_element_type=jnp.float32)
        denom = acc[SWA_HEAD_DIM:SWA_HEAD_DIM + 1] + jnp.exp2(sinks[kv] - m)
        ot = acc[:SWA_HEAD_DIM] / denom
        for g in range(SWA_GROUP):
            hd = kv * SWA_GROUP + g
            o_ref[qrows, hd * SWA_HEAD_DIM:(hd + 1) * SWA_HEAD_DIM] = (
                ot[:, g * BLOCK:(g + 1) * BLOCK].T.astype(o_ref.dtype))

    n_units = SWA_KV_HEADS * nblk
    bufs = (sa_ref, sb_ref)
    m_prev = scores(0, bufs[0])
    for u in range(1, n_units):
        m_cur = scores(u, bufs[u % 2])
        attend(u - 1, bufs[(u - 1) % 2], m_prev)
        m_prev = m_cur
    attend(n_units - 1, bufs[(n_units - 1) % 2], m_prev)


def _swa_bias():
    r = jnp.arange(3 * BLOCK)[:, None]
    qi = jnp.arange(BLOCK)[None, :]
    dist = jnp.abs(qi + BLOCK - r)
    slopes = 2.0 ** (-8.0 * jnp.arange(1, SWA_HEADS + 1, dtype=jnp.float32) / SWA_HEADS)
    b = jnp.where(dist <= WINDOW, -(slopes * LOG2E)[:, None, None] * dist.astype(jnp.float32),
                  -jnp.inf)
    b = b.reshape(SWA_KV_HEADS, SWA_GROUP, 3 * BLOCK, BLOCK).transpose(0, 2, 1, 3)
    return b.reshape(SWA_KV_HEADS, 3 * BLOCK, SWA_GROUP * BLOCK)


def _swa_attention(sink, qs, ks, vst, B, S):
    T = B * S
    tq = TQ_SWA
    nt = S // tq
    per = tq // BLOCK
    nb = S // BLOCK
    prev = lambda j: jnp.maximum(j * per - 1, 0)
    nxt = lambda j: jnp.minimum((j + 1) * per, nb - 1)
    return pl.pallas_call(
        functools.partial(_swa_kernel, seq=S),
        grid=(B, nt),
        in_specs=[
            pl.BlockSpec(memory_space=pltpu.SMEM),
            _resident((SWA_KV_HEADS, 3 * BLOCK, SWA_GROUP * BLOCK)),
            pl.BlockSpec((tq, N_SWA_Q), lambda b, j: (b * nt + j, 0)),
            pl.BlockSpec((BLOCK, N_SWA_KV), lambda b, j: (b * nb + prev(j), 0)),
            pl.BlockSpec((tq, N_SWA_KV), lambda b, j: (b * nt + j, 0)),
            pl.BlockSpec((BLOCK, N_SWA_KV), lambda b, j: (b * nb + nxt(j), 0)),
            pl.BlockSpec((None, N_SWA_KV, BLOCK), lambda b, j: (b, 0, prev(j))),
            pl.BlockSpec((None, N_SWA_KV, tq), lambda b, j: (b, 0, j)),
            pl.BlockSpec((None, N_SWA_KV, BLOCK), lambda b, j: (b, 0, nxt(j))),
        ],
        out_specs=pl.BlockSpec((tq, SWA_OUT), lambda b, j: (b * nt + j, 0)),
        out_shape=jax.ShapeDtypeStruct((T, SWA_OUT), jnp.bfloat16),
        scratch_shapes=[pltpu.VMEM((3 * BLOCK, SWA_GROUP * BLOCK), jnp.float32)] * 2,
        compiler_params=_params(2),
        name="swa",
    )(sink, _swa_bias(), qs, ks, ks, ks, vst, vst, vst)


def _out_kernel(oa_ref, ob_ref, x_ref, ga_ref, gb_ref, wo_ref, gpost_ref, x1_ref):
    oa = jnp.concatenate([oa_ref[hd] for hd in range(MLA_HEADS)], axis=1)
    na = _rms(oa.astype(jnp.float32), ga_ref[...]).astype(jnp.bfloat16)
    nb = _rms(ob_ref[...].astype(jnp.float32), gb_ref[...]).astype(jnp.bfloat16)
    o = jnp.dot(jnp.concatenate([na, nb], axis=1), wo_ref[...],
                preferred_element_type=jnp.float32)
    x1_ref[...] = x_ref[...] + _rms(o, gpost_ref[...])


def _out_projection(oa, ob, x2d, g_a, g_b, w_o, g_post):
    T = x2d.shape[0]
    tm = TM_OUT
    row = lambda i: (i, 0)
    return pl.pallas_call(
        _out_kernel,
        grid=(T // tm,),
        in_specs=[
            pl.BlockSpec((MLA_HEADS, tm, MLA_V), lambda i: (0, i, 0)),
            pl.BlockSpec((tm, SWA_OUT), row),
            pl.BlockSpec((tm, D_MODEL), row),
            _resident((1, MLA_OUT)),
            _resident((1, SWA_OUT)),
            _resident((MLA_OUT + SWA_OUT, D_MODEL)),
            _resident((1, D_MODEL)),
        ],
        out_specs=pl.BlockSpec((tm, D_MODEL), row),
        out_shape=jax.ShapeDtypeStruct((T, D_MODEL), jnp.float32),
        compiler_params=_params(1),
        name="outproj",
    )(oa, ob, x2d, g_a, g_b, w_o, g_post)


def _ffn_kernel(x1_ref, gpre_ref, wg_ref, wu_ref, wd_ref, gpost_ref, y_ref, h_ref):
    j = pl.program_id(1)
    last_j = pl.num_programs(1) - 1

    def step(first, last):
        for r0 in range(0, h_ref.shape[0], SUB_FFN):
            rows = slice(r0, r0 + SUB_FFN)
            if first:
                h = _rms(x1_ref[rows, :], gpre_ref[...]).astype(h_ref.dtype)
                h_ref[rows, :] = h
            else:
                h = h_ref[rows, :]
            gate = jnp.dot(h, wg_ref[...], preferred_element_type=jnp.float32)
            up = jnp.dot(h, wu_ref[...], preferred_element_type=jnp.float32)
            act = (gate * jax.nn.sigmoid(gate) * up).astype(jnp.bfloat16)
            acc = jnp.dot(act, wd_ref[...], preferred_element_type=jnp.float32)
            if not first:
                acc = y_ref[rows, :] + acc
            if last:
                acc = x1_ref[rows, :] + _rms(acc, gpost_ref[...])
            y_ref[rows, :] = acc

    pl.when(j == 0)(functools.partial(step, True, False))
    pl.when((j > 0) & (j < last_j))(functools.partial(step, False, False))
    pl.when(j == last_j)(functools.partial(step, False, True))


def _ffn(x1, g_pre, w_gate, w_up, w_down, g_post):
    T = x1.shape[0]
    tm, tf = TM_FFN, TF_FFN
    assert D_FF // tf >= 2, "first and last d_ff chunk must be different steps"
    row = lambda i, j: (i, 0)
    return pl.pallas_call(
        _ffn_kernel,
        grid=(T // tm, D_FF // tf),
        in_specs=[
            pl.BlockSpec((tm, D_MODEL), row),
            _resident((1, D_MODEL)),
            pl.BlockSpec((D_MODEL, tf), lambda i, j: (0, j)),
            pl.BlockSpec((D_MODEL, tf), lambda i, j: (0, j)),
            pl.BlockSpec((tf, D_MODEL), lambda i, j: (j, 0)),
            _resident((1, D_MODEL)),
        ],
        out_specs=pl.BlockSpec((tm, D_MODEL), row),
        out_shape=jax.ShapeDtypeStruct((T, D_MODEL), jnp.float32),
        scratch_shapes=[pltpu.VMEM((tm, D_MODEL), jnp.bfloat16)],
        compiler_params=_params(2),
        name="ffn",
    )(x1, g_pre, w_gate, w_up, w_down, g_post)


def _rope_tables(S):
    inv = ROPE_THETA ** (-jnp.arange(0, MLA_ROPE, 2, dtype=jnp.float32) / MLA_ROPE)
    ang = jnp.arange(S, dtype=jnp.float32)[:, None] * inv[None, :]
    c, s = jnp.cos(ang), jnp.sin(ang)
    z = jnp.zeros_like(c)
    cos_t = jnp.concatenate([c, c, c, c], axis=1)
    sin_a = jnp.concatenate([-s, z, -s, z], axis=1)
    sin_b = jnp.concatenate([z, s, z, s], axis=1)
    return cos_t, sin_a, sin_b


def _prep_weights(w_in, w_uk, w_uv, w_o, w_gate, w_up, w_down):
    bf = jnp.bfloat16
    wq = w_in[:, :N_MLA_Q].reshape(D_MODEL, MLA_HEADS, MLA_NOPE + MLA_ROPE)
    w_nope = wq[:, :, :MLA_NOPE].reshape(D_MODEL, MLA_HEADS * MLA_NOPE)
    w_qpe = wq[:, :, MLA_NOPE:].reshape(D_MODEL, MLA_HEADS * MLA_ROPE)
    o = N_MLA_Q
    w_ckv = w_in[:, o:o + KV_RANK]; o += KV_RANK
    w_kpe = jnp.pad(w_in[:, o:o + MLA_ROPE], ((0, 0), (0, LANES - MLA_ROPE))); o += MLA_ROPE
    w_swa = w_in[:, o:]
    w1 = jnp.concatenate([w_nope, w_qpe, w_swa, w_ckv, w_kpe], axis=1).astype(bf)
    assert w1.shape == (D_MODEL, N_PROJ)
    return dict(
        w1=w1,
        w_uk=w_uk.reshape(KV_RANK, MLA_HEADS * MLA_NOPE).astype(bf),
        w_uvt=w_uv.reshape(KV_RANK, MLA_OUT).T.astype(bf),
        w_o=w_o.astype(bf), w_gate=w_gate.astype(bf), w_up=w_up.astype(bf),
        w_down=w_down.astype(bf))


def _layer(x, w, g_pre_attn, g_ckv, sink, g_out_mla, g_out_swa, g_post_attn, g_pre_ffn, g_post_ffn):
    B, S, _ = x.shape
    x2d = x.reshape(B * S, D_MODEL)
    cos_t, sin_a, sin_b = _rope_tables(S)
    q, k, vt, qs, ks, vst = _projection(x2d, B, S, g_pre_attn, w["w1"], g_ckv, w["w_uk"],
                                        w["w_uvt"], cos_t, sin_a, sin_b)
    oa = _mla_attention(q, k, vt, B, S)
    ob = _swa_attention(sink, qs, ks, vst, B, S)
    x1 = _out_projection(oa, ob, x2d, g_out_mla, g_out_swa, w["w_o"], g_post_attn)
    y = _ffn(x1, g_pre_ffn, w["w_gate"], w["w_up"], w["w_down"], g_post_ffn)
    return y.reshape(B, S, D_MODEL)


def kernel(x_prompt, x_sample, g_pre_attn, w_in, g_ckv, w_uk, w_uv, sink, g_out_mla, g_out_swa,
           w_o, g_post_attn, g_pre_ffn, w_gate, w_up, w_down, g_post_ffn):
    assert w_in.shape[0] == 1, "single layer"
    w = _prep_weights(w_in[0], w_uk[0], w_uv[0], w_o[0], w_gate[0], w_up[0], w_down[0])
    gains = (g_pre_attn, g_ckv, sink[0], g_out_mla, g_out_swa, g_post_attn, g_pre_ffn, g_post_ffn)
    return (_layer(x_prompt, w, *gains), _layer(x_sample, w, *gains))
```

```python
import functools
import math

import jax
import jax.numpy as jnp
from jax import lax
from jax.experimental import pallas as pl
from jax.experimental.pallas import tpu as pltpu

D_MODEL = 2048
MLA_HEADS = 8
MLA_NOPE = 128
MLA_ROPE = 64
MLA_V = 128
KV_RANK = 512
ROPE_THETA = 10000.0
SWA_HEADS = 8
SWA_KV_HEADS = 2
SWA_HEAD_DIM = 128
SWA_GROUP = SWA_HEADS // SWA_KV_HEADS
WINDOW = 128
BLOCK = 128
D_FF = -(-8 * D_MODEL // (3 * 256)) * 256
EPS = 1e-6

N_MLA_Q = MLA_HEADS * (MLA_NOPE + MLA_ROPE)
N_SWA_Q = SWA_HEADS * SWA_HEAD_DIM
N_SWA_KV = SWA_KV_HEADS * SWA_HEAD_DIM
MLA_OUT = MLA_HEADS * MLA_V
SWA_OUT = SWA_HEADS * SWA_HEAD_DIM

LANES = 128
QK_PAD = 2 * LANES
OFF_QPE = MLA_HEADS * MLA_NOPE
OFF_QS = OFF_QPE + MLA_HEADS * MLA_ROPE
OFF_KS = OFF_QS + N_SWA_Q
OFF_VS = OFF_KS + N_SWA_KV
OFF_CKV = OFF_VS + N_SWA_KV
OFF_KPE = OFF_CKV + KV_RANK
N_PROJ = OFF_KPE + LANES

LOG2E = math.log2(math.e)
MLA_SCALE = (MLA_NOPE + MLA_ROPE) ** -0.5
SWA_SCALE = SWA_HEAD_DIM ** -0.5

VMEM_LIMIT = 56 * 1024 * 1024

TM_PROJ = 512
TQ_MLA = 256
QT_MLA = 2
BF16_SUBLANES = 16
ONES_ROWS = BF16_SUBLANES
V_ROWS = MLA_V + ONES_ROWS
CK_MLA = 512
TQ_SWA = 512
TM_OUT = 512
TM_FFN = 1024
TF_FFN = 512
SUB_FFN = 512

_NT = (((1,), (1,)), ((), ()))


def _params(n_axes):
    return pltpu.CompilerParams(
        dimension_semantics=("arbitrary",) * n_axes, vmem_limit_bytes=VMEM_LIMIT)


def _resident(shape):
    return pl.BlockSpec(shape, lambda *_: (0,) * len(shape), pipeline_mode=pl.Buffered(1))


def _rms(x, g):
    var = jnp.mean(x * x, axis=-1, keepdims=True)
    return x * lax.rsqrt(var + EPS) * g


def _rope_pairs(x, cos_t, sin_a, sin_b):
    half = MLA_ROPE // 2
    return x * cos_t + pltpu.roll(x, LANES - half, 1) * sin_a + pltpu.roll(x, half, 1) * sin_b


def _proj_kernel(*refs, n_cast):
    (x_ref, g_ref, w1_ref, gckv_ref, wuk_ref, wuvt_ref, cos_ref, sina_ref, sinb_ref) = refs[:9]
    q_ref, k_ref, vt_ref, qs_ref, ks_ref, vst_ref = refs[9 + n_cast:15 + n_cast]
    for src_ref, dst_ref in zip(refs[9:9 + n_cast], refs[15 + n_cast:]):
        dst_ref[...] = src_ref[...].astype(dst_ref.dtype)

    h = _rms(x_ref[...], g_ref[...]).astype(jnp.bfloat16)
    cos_t, sin_a, sin_b = cos_ref[...], sina_ref[...], sinb_ref[...]
    bf = jnp.bfloat16

    def proj(lo, hi):
        return jnp.dot(h, w1_ref[:, lo:hi], preferred_element_type=jnp.float32)

    low_half = lax.broadcasted_iota(jnp.int32, cos_t.shape, 1) < MLA_ROPE
    q_scale = MLA_SCALE * LOG2E
    q_pe = proj(OFF_QPE, OFF_QS) * q_scale
    for pair in range(MLA_HEADS // 2):
        nope = proj(pair * 2 * MLA_NOPE, (pair + 1) * 2 * MLA_NOPE) * q_scale
        rot = _rope_pairs(q_pe[:, pair * LANES:(pair + 1) * LANES], cos_t, sin_a, sin_b)
        q_ref[2 * pair, :, :LANES] = nope[:, :MLA_NOPE].astype(bf)
        q_ref[2 * pair, :, LANES:] = jnp.where(low_half, rot, 0.0).astype(bf)
        q_ref[2 * pair + 1, :, :LANES] = nope[:, MLA_NOPE:].astype(bf)
        q_ref[2 * pair + 1, :, LANES:] = jnp.where(low_half, pltpu.roll(rot, MLA_ROPE, 1), 0.0).astype(bf)

    qs_ref[...] = (proj(OFF_QS, OFF_KS) * (SWA_SCALE * LOG2E)).astype(bf)
    kvs = proj(OFF_KS, OFF_CKV)
    ks_ref[...] = kvs[:, :N_SWA_KV].astype(bf)
    vst_ref[...] = kvs[:, N_SWA_KV:].T.astype(bf)

    lat = proj(OFF_CKV, N_PROJ)
    c_kv = _rms(lat[:, :KV_RANK], gckv_ref[...]).astype(bf)
    kpe = _rope_pairs(lat[:, KV_RANK:], cos_t, sin_a, sin_b).astype(bf)
    k_nope = jnp.dot(c_kv, wuk_ref[...], preferred_element_type=jnp.float32)
    for hd in range(MLA_HEADS):
        k_ref[hd, :, :LANES] = k_nope[:, hd * MLA_NOPE:(hd + 1) * MLA_NOPE].astype(bf)
        k_ref[hd, :, LANES:] = kpe
    vt = lax.dot_general(wuvt_ref[...], c_kv, _NT, preferred_element_type=jnp.float32)
    for hd in range(MLA_HEADS):
        vt_ref[hd, :MLA_V, :] = vt[hd * MLA_V:(hd + 1) * MLA_V, :].astype(bf)
        vt_ref[hd, MLA_V:, :] = jnp.ones((ONES_ROWS, vt.shape[1]), bf)


def _cast_spec(shape, steps):
    rows, cols = shape
    group = 1
    while (rows * group) % steps or (rows * group // steps) % BF16_SUBLANES:
        group *= 2
    return pl.BlockSpec((rows * group // steps, cols), lambda i: (i // group, 0))


def _projection(x2d, B, S, g_pre, w1, g_ckv, w_uk, w_uvt, cos_t, sin_a, sin_b, cast=()):
    T = B * S
    tm = TM_PROJ
    spt = S // tm
    row = lambda i: (i, 0)
    tab = pl.BlockSpec((tm, LANES), lambda i: (i % spt, 0))
    tposed = lambda n: pl.BlockSpec((None, n, tm), lambda i: (i // spt, 0, i % spt))
    bf = jnp.bfloat16
    cast_specs = [_cast_spec(c.shape, T // tm) for c in cast]
    return pl.pallas_call(
        functools.partial(_proj_kernel, n_cast=len(cast)),
        grid=(T // tm,),
        in_specs=[
            pl.BlockSpec((tm, D_MODEL), row),
            _resident((1, D_MODEL)),
            _resident((D_MODEL, N_PROJ)),
            _resident((1, KV_RANK)),
            _resident((KV_RANK, MLA_HEADS * MLA_NOPE)),
            _resident((MLA_OUT, KV_RANK)),
            tab, tab, tab,
        ] + cast_specs,
        out_specs=[
            pl.BlockSpec((MLA_HEADS, tm, QK_PAD), lambda i: (0, i, 0)),
            pl.BlockSpec((MLA_HEADS, tm, QK_PAD), lambda i: (0, i, 0)),
            pl.BlockSpec((None, MLA_HEADS, V_ROWS, tm), lambda i: (i // spt, 0, 0, i % spt)),
            pl.BlockSpec((tm, N_SWA_Q), row),
            pl.BlockSpec((tm, N_SWA_KV), row),
            tposed(N_SWA_KV),
        ] + cast_specs,
        out_shape=[
            jax.ShapeDtypeStruct((MLA_HEADS, T, QK_PAD), bf),
            jax.ShapeDtypeStruct((MLA_HEADS, T, QK_PAD), bf),
            jax.ShapeDtypeStruct((B, MLA_HEADS, V_ROWS, S), bf),
            jax.ShapeDtypeStruct((T, N_SWA_Q), bf),
            jax.ShapeDtypeStruct((T, N_SWA_KV), bf),
            jax.ShapeDtypeStruct((B, N_SWA_KV, S), bf),
        ] + [jax.ShapeDtypeStruct(c.shape, bf) for c in cast],
        compiler_params=_params(1),
        name="proj",
    )(x2d, g_pre, w1, g_ckv, w_uk, w_uvt, cos_t, sin_a, sin_b, *cast)


def _mla_kernel(q_ref, k_ref, vt_ref, o_ref, sa_ref, sb_ref):
    seq = k_ref.shape[1]
    tq = sa_ref.shape[1]
    n_units = MLA_HEADS * (q_ref.shape[1] // tq)
    chunks = [slice(c * CK_MLA, (c + 1) * CK_MLA) for c in range(seq // CK_MLA)]

    def unit(u):
        hd, qt = u % MLA_HEADS, u // MLA_HEADS
        return hd, slice(qt * tq, (qt + 1) * tq)

    def scores(u, s_ref):
        hd, rows = unit(u)
        q = q_ref[hd, rows, :]
        m = None
        for ck in chunks:
            s = lax.dot_general(k_ref[hd, ck, :], q, _NT, preferred_element_type=jnp.float32)
            s_ref[ck, :] = s
            cm = jnp.max(s, axis=0, keepdims=True)
            m = cm if m is None else jnp.maximum(m, cm)
        return m

    def attend(u, s_ref, m):
        hd, rows = unit(u)
        acc = jnp.zeros((V_ROWS, tq), jnp.float32)
        for ck in chunks:
            p = jnp.exp2(s_ref[ck, :] - m)
            acc = acc + jnp.dot(vt_ref[hd, :, ck], p.astype(jnp.bfloat16),
                                preferred_element_type=jnp.float32)
        o_ref[hd, rows, :] = (acc[:MLA_V] / acc[MLA_V:MLA_V + 1]).T.astype(o_ref.dtype)

    bufs = (sa_ref, sb_ref)
    m_prev = scores(0, bufs[0])
    for u in range(1, n_units):
        m_cur = scores(u, bufs[u % 2])
        attend(u - 1, bufs[(u - 1) % 2], m_prev)
        m_prev = m_cur
    attend(n_units - 1, bufs[(n_units - 1) % 2], m_prev)


def _mla_attention(q, k, vt, B, S):
    T = B * S
    rows = TQ_MLA * QT_MLA
    nq = S // rows
    kv_bytes = S * MLA_HEADS * (QK_PAD + V_ROWS) * 2
    mode = dict(pipeline_mode=pl.Buffered(1)) if 2 * kv_bytes > VMEM_LIMIT // 2 else {}
    return pl.pallas_call(
        _mla_kernel,
        grid=(B, nq),
        in_specs=[
            pl.BlockSpec((MLA_HEADS, rows, QK_PAD), lambda b, i: (0, b * nq + i, 0)),
            pl.BlockSpec((MLA_HEADS, S, QK_PAD), lambda b, i: (0, b, 0), **mode),
            pl.BlockSpec((None, MLA_HEADS, V_ROWS, S), lambda b, i: (b, 0, 0, 0), **mode),
        ],
        out_specs=pl.BlockSpec((MLA_HEADS, rows, MLA_V), lambda b, i: (0, b * nq + i, 0)),
        out_shape=jax.ShapeDtypeStruct((MLA_HEADS, T, MLA_V), jnp.bfloat16),
        scratch_shapes=[pltpu.VMEM((S, TQ_MLA), jnp.float32), pltpu.VMEM((S, TQ_MLA), jnp.float32)],
        compiler_params=_params(2),
        name="mla",
    )(q, k, vt)


def _swa_kernel(sink_ref, bias_ref, q_ref, kp_ref, kc_ref, kn_ref, vp_ref, vc_ref, vn_ref, o_ref,
                sa_ref, sb_ref, *, seq):
    i = pl.program_id(1)
    tq = q_ref.shape[0]
    nblk = tq // BLOCK
    kcat = jnp.concatenate([kp_ref[...], kc_ref[...], kn_ref[...]], axis=0)
    vcat = jnp.concatenate([vp_ref[...], vc_ref[...], vn_ref[...]], axis=1)
    slot = lax.broadcasted_iota(jnp.int32, (3 * BLOCK, 1), 0)
    ones = jnp.ones((ONES_ROWS, 3 * BLOCK), jnp.bfloat16)
    sinks = [jnp.concatenate(
        [jnp.full((1, BLOCK), sink_ref[kv * SWA_GROUP + g] * LOG2E, jnp.float32)
         for g in range(SWA_GROUP)], axis=1) for kv in range(SWA_KV_HEADS)]

    def unit(u):
        kv, jb = u // nblk, u % nblk
        dsl = slice(kv * SWA_HEAD_DIM, (kv + 1) * SWA_HEAD_DIM)
        win = slice(jb * BLOCK, (jb + 3) * BLOCK)
        return kv, jb, dsl, win, slice(jb * BLOCK, (jb + 1) * BLOCK)

    def scores(u, s_ref):
        kv, jb, dsl, win, qrows = unit(u)
        qg = jnp.concatenate(
            [q_ref[qrows, (kv * SWA_GROUP + g) * SWA_HEAD_DIM:(kv * SWA_GROUP + g + 1) * SWA_HEAD_DIM]
             for g in range(SWA_GROUP)], axis=0)
        s = lax.dot_general(kcat[win, dsl], qg, _NT, preferred_element_type=jnp.float32)
        s = s + bias_ref[kv]
        if jb == 0 or jb == nblk - 1:
            key_pos = i * tq + (jb - 1) * BLOCK + slot
            s = jnp.where((key_pos >= 0) & (key_pos < seq), s, -jnp.inf)
        s_ref[...] = s
        return jnp.maximum(jnp.max(s, axis=0, keepdims=True), sinks[kv])

    def attend(u, s_ref, m):
        kv, jb, dsl, win, qrows = unit(u)
        p = jnp.exp2(s_ref[...] - m).astype(jnp.bfloat16)
        acc = jnp.dot(jnp.concatenate([vcat[dsl, win], ones], axis=0), p,
                      preferred_element_type=jnp.float32)
        denom = acc[SWA_HEAD_DIM:SWA_HEAD_DIM + 1] + jnp.exp2(sinks[kv] - m)
        ot = acc[:SWA_HEAD_DIM] / denom
        for g in range(SWA_GROUP):
            hd = kv * SWA_GROUP + g
            o_ref[qrows, hd * SWA_HEAD_DIM:(hd + 1) * SWA_HEAD_DIM] = (
                ot[:, g * BLOCK:(g + 1) * BLOCK].T.astype(o_ref.dtype))

    n_units = SWA_KV_HEADS * nblk
    bufs = (sa_ref, sb_ref)
    m_prev = scores(0, bufs[0])
    for u in range(1, n_units):
        m_cur = scores(u, bufs[u % 2])
        attend(u - 1, bufs[(u - 1) % 2], m_prev)
        m_prev = m_cur
    attend(n_units - 1, bufs[(n_units - 1) % 2], m_prev)


def _swa_bias():
    r = jnp.arange(3 * BLOCK)[:, None]
    qi = jnp.arange(BLOCK)[None, :]
    dist = jnp.abs(qi + BLOCK - r)
    slopes = 2.0 ** (-8.0 * jnp.arange(1, SWA_HEADS + 1, dtype=jnp.float32) / SWA_HEADS)
    b = jnp.where(dist <= WINDOW, -(slopes * LOG2E)[:, None, None] * dist.astype(jnp.float32),
                  -jnp.inf)
    b = b.reshape(SWA_KV_HEADS, SWA_GROUP, 3 * BLOCK, BLOCK).transpose(0, 2, 1, 3)
    return b.reshape(SWA_KV_HEADS, 3 * BLOCK, SWA_GROUP * BLOCK)


def _swa_attention(sink, qs, ks, vst, B, S):
    T = B * S
    tq = TQ_SWA
    nt = S // tq
    per = tq // BLOCK
    nb = S // BLOCK
    prev = lambda j: jnp.maximum(j * per - 1, 0)
    nxt = lambda j: jnp.minimum((j + 1) * per, nb - 1)
    return pl.pallas_call(
        functools.partial(_swa_kernel, seq=S),
        grid=(B, nt),
        in_specs=[
            pl.BlockSpec(memory_space=pltpu.SMEM),
            _resident((SWA_KV_HEADS, 3 * BLOCK, SWA_GROUP * BLOCK)),
            pl.BlockSpec((tq, N_SWA_Q), lambda b, j: (b * nt + j, 0)),
            pl.BlockSpec((BLOCK, N_SWA_KV), lambda b, j: (b * nb + prev(j), 0)),
            pl.BlockSpec((tq, N_SWA_KV), lambda b, j: (b * nt + j, 0)),
            pl.BlockSpec((BLOCK, N_SWA_KV), lambda b, j: (b * nb + nxt(j), 0)),
            pl.BlockSpec((None, N_SWA_KV, BLOCK), lambda b, j: (b, 0, prev(j))),
            pl.BlockSpec((None, N_SWA_KV, tq), lambda b, j: (b, 0, j)),
            pl.BlockSpec((None, N_SWA_KV, BLOCK), lambda b, j: (b, 0, nxt(j))),
        ],
        out_specs=pl.BlockSpec((tq, SWA_OUT), lambda b, j: (b * nt + j, 0)),
        out_shape=jax.ShapeDtypeStruct((T, SWA_OUT), jnp.bfloat16),
        scratch_shapes=[pltpu.VMEM((3 * BLOCK, SWA_GROUP * BLOCK), jnp.float32)] * 2,
        compiler_params=_params(2),
        name="swa",
    )(sink, _swa_bias(), qs, ks, ks, ks, vst, vst, vst)


def _out_kernel(oa_ref, ob_ref, x_ref, ga_ref, gb_ref, wo_ref, gpost_ref, x1_ref):
    oa = jnp.concatenate([oa_ref[hd] for hd in range(MLA_HEADS)], axis=1)
    na = _rms(oa.astype(jnp.float32), ga_ref[...]).astype(jnp.bfloat16)
    nb = _rms(ob_ref[...].astype(jnp.float32), gb_ref[...]).astype(jnp.bfloat16)
    o = jnp.dot(jnp.concatenate([na, nb], axis=1), wo_ref[...],
                preferred_element_type=jnp.float32)
    x1_ref[...] = x_ref[...] + _rms(o, gpost_ref[...])


def _out_projection(oa, ob, x2d, g_a, g_b, w_o, g_post):
    T = x2d.shape[0]
    tm = TM_OUT
    row = lambda i: (i, 0)
    return pl.pallas_call(
        _out_kernel,
        grid=(T // tm,),
        in_specs=[
            pl.BlockSpec((MLA_HEADS, tm, MLA_V), lambda i: (0, i, 0)),
            pl.BlockSpec((tm, SWA_OUT), row),
            pl.BlockSpec((tm, D_MODEL), row),
            _resident((1, MLA_OUT)),
            _resident((1, SWA_OUT)),
            _resident((MLA_OUT + SWA_OUT, D_MODEL)),
            _resident((1, D_MODEL)),
        ],
        out_specs=pl.BlockSpec((tm, D_MODEL), row),
        out_shape=jax.ShapeDtypeStruct((T, D_MODEL), jnp.float32),
        compiler_params=_params(1),
        name="outproj",
    )(oa, ob, x2d, g_a, g_b, w_o, g_post)


def _ffn_kernel(x1_ref, gpre_ref, wg_ref, wu_ref, wd_ref, gpost_ref, y_ref, h_ref):
    j = pl.program_id(1)
    last_j = pl.num_programs(1) - 1

    def step(first, last):
        for r0 in range(0, h_ref.shape[0], SUB_FFN):
            rows = slice(r0, r0 + SUB_FFN)
            if first:
                h = _rms(x1_ref[rows, :], gpre_ref[...]).astype(h_ref.dtype)
                h_ref[rows, :] = h
            else:
                h = h_ref[rows, :]
            gate = jnp.dot(h, wg_ref[...], preferred_element_type=jnp.float32)
            up = jnp.dot(h, wu_ref[...], preferred_element_type=jnp.float32)
            act = (gate * jax.nn.sigmoid(gate) * up).astype(jnp.bfloat16)
            acc = jnp.dot(act, wd_ref[...], preferred_element_type=jnp.float32)
            if not first:
                acc = y_ref[rows, :] + acc
            if last:
                acc = x1_ref[rows, :] + _rms(acc, gpost_ref[...])
            y_ref[rows, :] = acc

    pl.when(j == 0)(functools.partial(step, True, False))
    pl.when((j > 0) & (j < last_j))(functools.partial(step, False, False))
    pl.when(j == last_j)(functools.partial(step, False, True))


def _ffn(x1, g_pre, w_gate, w_up, w_down, g_post):
    T = x1.shape[0]
    tm, tf = TM_FFN, TF_FFN
    assert D_FF // tf >= 2, "first and last d_ff chunk must be different steps"
    row = lambda i, j: (i, 0)
    return pl.pallas_call(
        _ffn_kernel,
        grid=(T // tm, D_FF // tf),
        in_specs=[
            pl.BlockSpec((tm, D_MODEL), row),
            _resident((1, D_MODEL)),
            pl.BlockSpec((D_MODEL, tf), lambda i, j: (0, j)),
            pl.BlockSpec((D_MODEL, tf), lambda i, j: (0, j)),
            pl.BlockSpec((tf, D_MODEL), lambda i, j: (j, 0)),
            _resident((1, D_MODEL)),
        ],
        out_specs=pl.BlockSpec((tm, D_MODEL), row),
        out_shape=jax.ShapeDtypeStruct((T, D_MODEL), jnp.float32),
        scratch_shapes=[pltpu.VMEM((tm, D_MODEL), jnp.bfloat16)],
        compiler_params=_params(2),
        name="ffn",
    )(x1, g_pre, w_gate, w_up, w_down, g_post)


def _rope_tables(S):
    inv = ROPE_THETA ** (-jnp.arange(0, MLA_ROPE, 2, dtype=jnp.float32) / MLA_ROPE)
    ang = jnp.arange(S, dtype=jnp.float32)[:, None] * inv[None, :]
    c, s = jnp.cos(ang), jnp.sin(ang)
    z = jnp.zeros_like(c)
    cos_t = jnp.concatenate([c, c, c, c], axis=1)
    sin_a = jnp.concatenate([-s, z, -s, z], axis=1)
    sin_b = jnp.concatenate([z, s, z, s], axis=1)
    return cos_t, sin_a, sin_b


def _prep_weights(w_in, w_uk, w_uv, w_o):
    bf = jnp.bfloat16
    wq = w_in[:, :N_MLA_Q].reshape(D_MODEL, MLA_HEADS, MLA_NOPE + MLA_ROPE)
    w_nope = wq[:, :, :MLA_NOPE].reshape(D_MODEL, MLA_HEADS * MLA_NOPE)
    w_qpe = wq[:, :, MLA_NOPE:].reshape(D_MODEL, MLA_HEADS * MLA_ROPE)
    o = N_MLA_Q
    w_ckv = w_in[:, o:o + KV_RANK]; o += KV_RANK
    w_kpe = jnp.pad(w_in[:, o:o + MLA_ROPE], ((0, 0), (0, LANES - MLA_ROPE))); o += MLA_ROPE
    w_swa = w_in[:, o:]
    w1 = jnp.concatenate([w_nope, w_qpe, w_swa, w_ckv, w_kpe], axis=1).astype(bf)
    assert w1.shape == (D_MODEL, N_PROJ)
    return dict(
        w1=w1,
        w_uk=w_uk.reshape(KV_RANK, MLA_HEADS * MLA_NOPE).astype(bf),
        w_uvt=w_uv.reshape(KV_RANK, MLA_OUT).T.astype(bf),
        w_o=w_o.astype(bf))


def _project(x, w, g_pre_attn, g_ckv, cast=()):
    B, S, _ = x.shape
    return _projection(x.reshape(B * S, D_MODEL), B, S, g_pre_attn, w["w1"], g_ckv, w["w_uk"],
                       w["w_uvt"], *_rope_tables(S), cast=cast)


def _mix_and_ffn(x, proj, w, ffn_w, sink, g_out_mla, g_out_swa, g_post_attn, g_pre_ffn, g_post_ffn):
    B, S, _ = x.shape
    q, k, vt, qs, ks, vst = proj
    oa = _mla_attention(q, k, vt, B, S)
    ob = _swa_attention(sink, qs, ks, vst, B, S)
    x1 = _out_projection(oa, ob, x.reshape(B * S, D_MODEL), g_out_mla, g_out_swa, w["w_o"],
                         g_post_attn)
    return _ffn(x1, g_pre_ffn, *ffn_w, g_post_ffn).reshape(B, S, D_MODEL)


def kernel(x_prompt, x_sample, g_pre_attn, w_in, g_ckv, w_uk, w_uv, sink, g_out_mla, g_out_swa,
           w_o, g_post_attn, g_pre_ffn, w_gate, w_up, w_down, g_post_ffn):
    assert w_in.shape[0] == 1, "single layer"
    w = _prep_weights(w_in[0], w_uk[0], w_uv[0], w_o[0])
    proj_s = _project(x_sample, w, g_pre_attn, g_ckv, cast=(w_gate[0], w_up[0], w_down[0]))
    proj_p = _project(x_prompt, w, g_pre_attn, g_ckv)
    rest = (sink[0], g_out_mla, g_out_swa, g_post_attn, g_pre_ffn, g_post_ffn)
    return (_mix_and_ffn(x_prompt, proj_p, w, proj_s[6:], *rest),
            _mix_and_ffn(x_sample, proj_s[:6], w, proj_s[6:], *rest))
```

```python
import functools
import math

import jax
import jax.numpy as jnp
from jax import lax
from jax.experimental import pallas as pl
from jax.experimental.pallas import tpu as pltpu

D_MODEL = 2048
MLA_HEADS = 8
MLA_NOPE = 128
MLA_ROPE = 64
MLA_V = 128
KV_RANK = 512
ROPE_THETA = 10000.0
SWA_HEADS = 8
SWA_KV_HEADS = 2
SWA_HEAD_DIM = 128
SWA_GROUP = SWA_HEADS // SWA_KV_HEADS
WINDOW = 128
BLOCK = 128
D_FF = -(-8 * D_MODEL // (3 * 256)) * 256
EPS = 1e-6

N_MLA_Q = MLA_HEADS * (MLA_NOPE + MLA_ROPE)
N_SWA_Q = SWA_HEADS * SWA_HEAD_DIM
N_SWA_KV = SWA_KV_HEADS * SWA_HEAD_DIM
MLA_OUT = MLA_HEADS * MLA_V
SWA_OUT = SWA_HEADS * SWA_HEAD_DIM

LANES = 128
QK_PAD = 2 * LANES
OFF_QPE = MLA_HEADS * MLA_NOPE
OFF_QS = OFF_QPE + MLA_HEADS * MLA_ROPE
OFF_KS = OFF_QS + N_SWA_Q
OFF_VS = OFF_KS + N_SWA_KV
OFF_CKV = OFF_VS + N_SWA_KV
OFF_KPE = OFF_CKV + KV_RANK
N_PROJ = OFF_KPE + LANES

LOG2E = math.log2(math.e)
MLA_SCALE = (MLA_NOPE + MLA_ROPE) ** -0.5
SWA_SCALE = SWA_HEAD_DIM ** -0.5

VMEM_LIMIT = 56 * 1024 * 1024

TR_W1 = 256
TM_PROJ = 512
TQ_MLA = 256
QT_MLA = 2
BF16_SUBLANES = 16
ONES_ROWS = BF16_SUBLANES
V_ROWS = MLA_V + ONES_ROWS
CK_MLA = 512
TQ_SWA = 1024
TM_OUT = 512
TM_FFN = 1024
TF_FFN = 512
SUB_FFN = 512

_NT = (((1,), (1,)), ((), ()))


def _params(n_axes):
    return pltpu.CompilerParams(
        dimension_semantics=("arbitrary",) * n_axes, vmem_limit_bytes=VMEM_LIMIT)


def _resident(shape):
    return pl.BlockSpec(shape, lambda *_: (0,) * len(shape), pipeline_mode=pl.Buffered(1))


def _rms(x, g):
    var = jnp.mean(x * x, axis=-1, keepdims=True)
    return x * lax.rsqrt(var + EPS) * g


def _rope_pairs(x, cos_t, sin_a, sin_b):
    half = MLA_ROPE // 2
    return x * cos_t + pltpu.roll(x, LANES - half, 1) * sin_a + pltpu.roll(x, half, 1) * sin_b


def _proj_kernel(*refs, n_cast):
    (x_ref, g_ref, w1_ref, gckv_ref, wuk_ref, wuvt_ref, cos_ref, sina_ref, sinb_ref) = refs[:9]
    q_ref, k_ref, vt_ref, qs_ref, ks_ref, vst_ref = refs[9 + n_cast:15 + n_cast]
    for src_ref, dst_ref in zip(refs[9:9 + n_cast], refs[15 + n_cast:]):
        dst_ref[...] = src_ref[...].astype(dst_ref.dtype)

    h = _rms(x_ref[...], g_ref[...]).astype(jnp.bfloat16)
    cos_t, sin_a, sin_b = cos_ref[...], sina_ref[...], sinb_ref[...]
    bf = jnp.bfloat16

    def proj(lo, hi):
        return jnp.dot(h, w1_ref[:, lo:hi], preferred_element_type=jnp.float32)

    low_half = lax.broadcasted_iota(jnp.int32, cos_t.shape, 1) < MLA_ROPE
    q_scale = MLA_SCALE * LOG2E
    q_pe = proj(OFF_QPE, OFF_QS) * q_scale
    for pair in range(MLA_HEADS // 2):
        nope = proj(pair * 2 * MLA_NOPE, (pair + 1) * 2 * MLA_NOPE) * q_scale
        rot = _rope_pairs(q_pe[:, pair * LANES:(pair + 1) * LANES], cos_t, sin_a, sin_b)
        q_ref[2 * pair, :, :LANES] = nope[:, :MLA_NOPE].astype(bf)
        q_ref[2 * pair, :, LANES:] = jnp.where(low_half, rot, 0.0).astype(bf)
        q_ref[2 * pair + 1, :, :LANES] = nope[:, MLA_NOPE:].astype(bf)
        q_ref[2 * pair + 1, :, LANES:] = jnp.where(low_half, pltpu.roll(rot, MLA_ROPE, 1), 0.0).astype(bf)

    qs_ref[...] = (proj(OFF_QS, OFF_KS) * (SWA_SCALE * LOG2E)).astype(bf)
    kvs = proj(OFF_KS, OFF_CKV)
    ks_ref[...] = kvs[:, :N_SWA_KV].astype(bf)
    vst_ref[...] = kvs[:, N_SWA_KV:].T.astype(bf)

    lat = proj(OFF_CKV, N_PROJ)
    c_kv = _rms(lat[:, :KV_RANK], gckv_ref[...]).astype(bf)
    kpe = _rope_pairs(lat[:, KV_RANK:], cos_t, sin_a, sin_b).astype(bf)
    k_nope = jnp.dot(c_kv, wuk_ref[...], preferred_element_type=jnp.float32)
    for hd in range(MLA_HEADS):
        k_ref[hd, :, :LANES] = k_nope[:, hd * MLA_NOPE:(hd + 1) * MLA_NOPE].astype(bf)
        k_ref[hd, :, LANES:] = kpe
    vt = lax.dot_general(wuvt_ref[...], c_kv, _NT, preferred_element_type=jnp.float32)
    for hd in range(MLA_HEADS):
        vt_ref[hd, :MLA_V, :] = vt[hd * MLA_V:(hd + 1) * MLA_V, :].astype(bf)
        vt_ref[hd, MLA_V:, :] = jnp.ones((ONES_ROWS, vt.shape[1]), bf)


def _cast_spec(shape, steps):
    rows, cols = shape
    group = 1
    while (rows * group) % steps or (rows * group // steps) % BF16_SUBLANES:
        group *= 2
    return pl.BlockSpec((rows * group // steps, cols), lambda i: (i // group, 0))


def _projection(x2d, B, S, g_pre, w1, g_ckv, w_uk, w_uvt, cos_t, sin_a, sin_b, cast=()):
    T = B * S
    tm = TM_PROJ
    spt = S // tm
    row = lambda i: (i, 0)
    tab = pl.BlockSpec((tm, LANES), lambda i: (i % spt, 0))
    tposed = lambda n: pl.BlockSpec((None, n, tm), lambda i: (i // spt, 0, i % spt))
    bf = jnp.bfloat16
    cast_specs = [_cast_spec(c.shape, T // tm) for c in cast]
    return pl.pallas_call(
        functools.partial(_proj_kernel, n_cast=len(cast)),
        grid=(T // tm,),
        in_specs=[
            pl.BlockSpec((tm, D_MODEL), row),
            _resident((1, D_MODEL)),
            _resident((D_MODEL, N_PROJ)),
            _resident((1, KV_RANK)),
            _resident((KV_RANK, MLA_HEADS * MLA_NOPE)),
            _resident((MLA_OUT, KV_RANK)),
            tab, tab, tab,
        ] + cast_specs,
        out_specs=[
            pl.BlockSpec((MLA_HEADS, tm, QK_PAD), lambda i: (0, i, 0)),
            pl.BlockSpec((MLA_HEADS, tm, QK_PAD), lambda i: (0, i, 0)),
            pl.BlockSpec((None, MLA_HEADS, V_ROWS, tm), lambda i: (i // spt, 0, 0, i % spt)),
            pl.BlockSpec((tm, N_SWA_Q), row),
            pl.BlockSpec((tm, N_SWA_KV), row),
            tposed(N_SWA_KV),
        ] + cast_specs,
        out_shape=[
            jax.ShapeDtypeStruct((MLA_HEADS, T, QK_PAD), bf),
            jax.ShapeDtypeStruct((MLA_HEADS, T, QK_PAD), bf),
            jax.ShapeDtypeStruct((B, MLA_HEADS, V_ROWS, S), bf),
            jax.ShapeDtypeStruct((T, N_SWA_Q), bf),
            jax.ShapeDtypeStruct((T, N_SWA_KV), bf),
            jax.ShapeDtypeStruct((B, N_SWA_KV, S), bf),
        ] + [jax.ShapeDtypeStruct(c.shape, bf) for c in cast],
        compiler_params=_params(1),
        name="proj",
    )(x2d, g_pre, w1, g_ckv, w_uk, w_uvt, cos_t, sin_a, sin_b, *cast)


def _mla_kernel(q_ref, k_ref, vt_ref, o_ref, sa_ref, sb_ref):
    seq = k_ref.shape[1]
    tq = sa_ref.shape[1]
    n_units = MLA_HEADS * (q_ref.shape[1] // tq)
    chunks = [slice(c * CK_MLA, (c + 1) * CK_MLA) for c in range(seq // CK_MLA)]

    def unit(u):
        hd, qt = u % MLA_HEADS, u // MLA_HEADS
        return hd, slice(qt * tq, (qt + 1) * tq)

    def scores(u, s_ref):
        hd, rows = unit(u)
        q = q_ref[hd, rows, :]
        m = None
        for ck in chunks:
            s = lax.dot_general(k_ref[hd, ck, :], q, _NT, preferred_element_type=jnp.float32)
            s_ref[ck, :] = s
            cm = jnp.max(s, axis=0, keepdims=True)
            m = cm if m is None else jnp.maximum(m, cm)
        return m

    def attend(u, s_ref, m):
        hd, rows = unit(u)
        acc = jnp.zeros((V_ROWS, tq), jnp.float32)
        for ck in chunks:
            p = jnp.exp2(s_ref[ck, :] - m)
            acc = acc + jnp.dot(vt_ref[hd, :, ck], p.astype(jnp.bfloat16),
                                preferred_element_type=jnp.float32)
        o_ref[hd, rows, :] = (acc[:MLA_V] / acc[MLA_V:MLA_V + 1]).T.astype(o_ref.dtype)

    bufs = (sa_ref, sb_ref)
    m_prev = scores(0, bufs[0])
    for u in range(1, n_units):
        m_cur = scores(u, bufs[u % 2])
        attend(u - 1, bufs[(u - 1) % 2], m_prev)
        m_prev = m_cur
    attend(n_units - 1, bufs[(n_units - 1) % 2], m_prev)


def _mla_attention(q, k, vt, B, S):
    T = B * S
    rows = TQ_MLA * QT_MLA
    nq = S // rows
    kv_bytes = S * MLA_HEADS * (QK_PAD + V_ROWS) * 2
    mode = dict(pipeline_mode=pl.Buffered(1)) if 2 * kv_bytes > VMEM_LIMIT // 2 else {}
    return pl.pallas_call(
        _mla_kernel,
        grid=(B, nq),
        in_specs=[
            pl.BlockSpec((MLA_HEADS, rows, QK_PAD), lambda b, i: (0, b * nq + i, 0)),
            pl.BlockSpec((MLA_HEADS, S, QK_PAD), lambda b, i: (0, b, 0), **mode),
            pl.BlockSpec((None, MLA_HEADS, V_ROWS, S), lambda b, i: (b, 0, 0, 0), **mode),
        ],
        out_specs=pl.BlockSpec((MLA_HEADS, rows, MLA_V), lambda b, i: (0, b * nq + i, 0)),
        out_shape=jax.ShapeDtypeStruct((MLA_HEADS, T, MLA_V), jnp.bfloat16),
        scratch_shapes=[pltpu.VMEM((S, TQ_MLA), jnp.float32), pltpu.VMEM((S, TQ_MLA), jnp.float32)],
        compiler_params=_params(2),
        name="mla",
    )(q, k, vt)


def _swa_kernel(sink_ref, bias_ref, q_ref, kp_ref, kc_ref, kn_ref, vp_ref, vc_ref, vn_ref, o_ref,
                sa_ref, sb_ref, *, seq):
    i = pl.program_id(1)
    tq = q_ref.shape[0]
    nblk = tq // BLOCK
    kcat = jnp.concatenate([kp_ref[...], kc_ref[...], kn_ref[...]], axis=0)
    vcat = jnp.concatenate([vp_ref[...], vc_ref[...], vn_ref[...]], axis=1)
    slot = lax.broadcasted_iota(jnp.int32, (3 * BLOCK, 1), 0)
    ones = jnp.ones((ONES_ROWS, 3 * BLOCK), jnp.bfloat16)
    sinks = [jnp.concatenate(
        [jnp.full((1, BLOCK), sink_ref[kv * SWA_GROUP + g] * LOG2E, jnp.float32)
         for g in range(SWA_GROUP)], axis=1) for kv in range(SWA_KV_HEADS)]

    def unit(u):
        kv, jb = u // nblk, u % nblk
        dsl = slice(kv * SWA_HEAD_DIM, (kv + 1) * SWA_HEAD_DIM)
        win = slice(jb * BLOCK, (jb + 3) * BLOCK)
        return kv, jb, dsl, win, slice(jb * BLOCK, (jb + 1) * BLOCK)

    def scores(u, s_ref):
        kv, jb, dsl, win, qrows = unit(u)
        qg = jnp.concatenate(
            [q_ref[qrows, (kv * SWA_GROUP + g) * SWA_HEAD_DIM:(kv * SWA_GROUP + g + 1) * SWA_HEAD_DIM]
             for g in range(SWA_GROUP)], axis=0)
        s = lax.dot_general(kcat[win, dsl], qg, _NT, preferred_element_type=jnp.float32)
        s = s + bias_ref[kv]
        if jb == 0 or jb == nblk - 1:
            key_pos = i * tq + (jb - 1) * BLOCK + slot
            s = jnp.where((key_pos >= 0) & (key_pos < seq), s, -jnp.inf)
        s_ref[...] = s
        return jnp.maximum(jnp.max(s, axis=0, keepdims=True), sinks[kv])

    def attend(u, s_ref, m):
        kv, jb, dsl, win, qrows = unit(u)
        p = jnp.exp2(s_ref[...] - m).astype(jnp.bfloat16)
        acc = jnp.dot(jnp.concatenate([vcat[dsl, win], ones], axis=0), p,
                      preferred_element_type=jnp.float32)
        denom = acc[SWA_HEAD_DIM:SWA_HEAD_DIM + 1] + jnp.exp2(sinks[kv] - m)
        ot = acc[:SWA_HEAD_DIM] / denom
        for g in range(SWA_GROUP):
            hd = kv * SWA_GROUP + g
            o_ref[qrows, hd * SWA_HEAD_DIM:(hd + 1) * SWA_HEAD_DIM] = (
                ot[:, g * BLOCK:(g + 1) * BLOCK].T.astype(o_ref.dtype))

    n_units = SWA_KV_HEADS * nblk
    bufs = (sa_ref, sb_ref)
    m_prev = scores(0, bufs[0])
    for u in range(1, n_units):
        m_cur = scores(u, bufs[u % 2])
        attend(u - 1, bufs[(u - 1) % 2], m_prev)
        m_prev = m_cur
    attend(n_units - 1, bufs[(n_units - 1) % 2], m_prev)


def _swa_bias():
    r = jnp.arange(3 * BLOCK)[:, None]
    qi = jnp.arange(BLOCK)[None, :]
    dist = jnp.abs(qi + BLOCK - r)
    slopes = 2.0 ** (-8.0 * jnp.arange(1, SWA_HEADS + 1, dtype=jnp.float32) / SWA_HEADS)
    b = jnp.where(dist <= WINDOW, -(slopes * LOG2E)[:, None, None] * dist.astype(jnp.float32),
                  -jnp.inf)
    b = b.reshape(SWA_KV_HEADS, SWA_GROUP, 3 * BLOCK, BLOCK).transpose(0, 2, 1, 3)
    return b.reshape(SWA_KV_HEADS, 3 * BLOCK, SWA_GROUP * BLOCK)


def _swa_attention(sink, qs, ks, vst, B, S):
    T = B * S
    tq = TQ_SWA
    nt = S // tq
    per = tq // BLOCK
    nb = S // BLOCK
    prev = lambda j: jnp.maximum(j * per - 1, 0)
    nxt = lambda j: jnp.minimum((j + 1) * per, nb - 1)
    return pl.pallas_call(
        functools.partial(_swa_kernel, seq=S),
        grid=(B, nt),
        in_specs=[
            pl.BlockSpec(memory_space=pltpu.SMEM),
            _resident((SWA_KV_HEADS, 3 * BLOCK, SWA_GROUP * BLOCK)),
            pl.BlockSpec((tq, N_SWA_Q), lambda b, j: (b * nt + j, 0)),
            pl.BlockSpec((BLOCK, N_SWA_KV), lambda b, j: (b * nb + prev(j), 0)),
            pl.BlockSpec((tq, N_SWA_KV), lambda b, j: (b * nt + j, 0)),
            pl.BlockSpec((BLOCK, N_SWA_KV), lambda b, j: (b * nb + nxt(j), 0)),
            pl.BlockSpec((None, N_SWA_KV, BLOCK), lambda b, j: (b, 0, prev(j))),
            pl.BlockSpec((None, N_SWA_KV, tq), lambda b, j: (b, 0, j)),
            pl.BlockSpec((None, N_SWA_KV, BLOCK), lambda b, j: (b, 0, nxt(j))),
        ],
        out_specs=pl.BlockSpec((tq, SWA_OUT), lambda b, j: (b * nt + j, 0)),
        out_shape=jax.ShapeDtypeStruct((T, SWA_OUT), jnp.bfloat16),
        scratch_shapes=[pltpu.VMEM((3 * BLOCK, SWA_GROUP * BLOCK), jnp.float32)] * 2,
        compiler_params=_params(2),
        name="swa",
    )(sink, _swa_bias(), qs, ks, ks, ks, vst, vst, vst)


def _out_kernel(oa_ref, ob_ref, x_ref, ga_ref, gb_ref, wo_ref, gpost_ref, x1_ref):
    oa = jnp.concatenate([oa_ref[hd] for hd in range(MLA_HEADS)], axis=1)
    na = _rms(oa.astype(jnp.float32), ga_ref[...]).astype(jnp.bfloat16)
    nb = _rms(ob_ref[...].astype(jnp.float32), gb_ref[...]).astype(jnp.bfloat16)
    o = jnp.dot(jnp.concatenate([na, nb], axis=1), wo_ref[...],
                preferred_element_type=jnp.float32)
    x1_ref[...] = x_ref[...] + _rms(o, gpost_ref[...])


def _out_projection(oa, ob, x2d, g_a, g_b, w_o, g_post):
    T = x2d.shape[0]
    tm = TM_OUT
    row = lambda i: (i, 0)
    return pl.pallas_call(
        _out_kernel,
        grid=(T // tm,),
        in_specs=[
            pl.BlockSpec((MLA_HEADS, tm, MLA_V), lambda i: (0, i, 0)),
            pl.BlockSpec((tm, SWA_OUT), row),
            pl.BlockSpec((tm, D_MODEL), row),
            _resident((1, MLA_OUT)),
            _resident((1, SWA_OUT)),
            _resident((MLA_OUT + SWA_OUT, D_MODEL)),
            _resident((1, D_MODEL)),
        ],
        out_specs=pl.BlockSpec((tm, D_MODEL), row),
        out_shape=jax.ShapeDtypeStruct((T, D_MODEL), jnp.float32),
        compiler_params=_params(1),
        name="outproj",
    )(oa, ob, x2d, g_a, g_b, w_o, g_post)


def _ffn_kernel(x1_ref, gpre_ref, wg_ref, wu_ref, wd_ref, gpost_ref, y_ref, h_ref):
    j = pl.program_id(1)
    last_j = pl.num_programs(1) - 1

    def step(first, last):
        sub = SUB_FFN if (first or last) else h_ref.shape[0]
        for r0 in range(0, h_ref.shape[0], sub):
            rows = slice(r0, r0 + sub)
            if first:
                h = _rms(x1_ref[rows, :], gpre_ref[...]).astype(h_ref.dtype)
                h_ref[rows, :] = h
            else:
                h = h_ref[rows, :]
            gate = jnp.dot(h, wg_ref[...], preferred_element_type=jnp.float32)
            up = jnp.dot(h, wu_ref[...], preferred_element_type=jnp.float32)
            act = (gate * jax.nn.sigmoid(gate) * up).astype(jnp.bfloat16)
            if first or last:
                acc = jnp.dot(act, wd_ref[...], preferred_element_type=jnp.float32)
                if not first:
                    acc = y_ref[rows, :] + acc
                if last:
                    acc = x1_ref[rows, :] + _rms(acc, gpost_ref[...])
                y_ref[rows, :] = acc
            else:
                for c0 in range(0, D_MODEL, D_MODEL // 2):
                    cols = slice(c0, c0 + D_MODEL // 2)
                    y_ref[rows, cols] += jnp.dot(act, wd_ref[:, cols],
                                                 preferred_element_type=jnp.float32)

    pl.when(j == 0)(functools.partial(step, True, False))
    pl.when((j > 0) & (j < last_j))(functools.partial(step, False, False))
    pl.when(j == last_j)(functools.partial(step, False, True))


def _ffn(x1, g_pre, w_gate, w_up, w_down, g_post):
    T = x1.shape[0]
    tm, tf = TM_FFN, TF_FFN
    assert D_FF // tf >= 2, "first and last d_ff chunk must be different steps"
    row = lambda i, j: (i, 0)
    return pl.pallas_call(
        _ffn_kernel,
        grid=(T // tm, D_FF // tf),
        in_specs=[
            pl.BlockSpec((tm, D_MODEL), row),
            _resident((1, D_MODEL)),
            pl.BlockSpec((D_MODEL, tf), lambda i, j: (0, j)),
            pl.BlockSpec((D_MODEL, tf), lambda i, j: (0, j)),
            pl.BlockSpec((tf, D_MODEL), lambda i, j: (j, 0)),
            _resident((1, D_MODEL)),
        ],
        out_specs=pl.BlockSpec((tm, D_MODEL), row),
        out_shape=jax.ShapeDtypeStruct((T, D_MODEL), jnp.float32),
        scratch_shapes=[pltpu.VMEM((tm, D_MODEL), jnp.bfloat16)],
        compiler_params=_params(2),
        name="ffn",
    )(x1, g_pre, w_gate, w_up, w_down, g_post)


def _rope_tables(S):
    inv = ROPE_THETA ** (-jnp.arange(0, MLA_ROPE, 2, dtype=jnp.float32) / MLA_ROPE)
    ang = jnp.arange(S, dtype=jnp.float32)[:, None] * inv[None, :]
    c, s = jnp.cos(ang), jnp.sin(ang)
    z = jnp.zeros_like(c)
    cos_t = jnp.concatenate([c, c, c, c], axis=1)
    sin_a = jnp.concatenate([-s, z, -s, z], axis=1)
    sin_b = jnp.concatenate([z, s, z, s], axis=1)
    return cos_t, sin_a, sin_b


def _w1_kernel(w_ref, o_ref):
    w = w_ref[...]
    hw = MLA_NOPE + MLA_ROPE
    nope = [w[:, hd * hw:hd * hw + MLA_NOPE] for hd in range(MLA_HEADS)]
    qpe = [w[:, hd * hw + MLA_NOPE:(hd + 1) * hw] for hd in range(MLA_HEADS)]
    o = N_MLA_Q
    ckv = w[:, o:o + KV_RANK]
    kpe = w[:, o + KV_RANK:o + KV_RANK + MLA_ROPE]
    swa = w[:, o + KV_RANK + MLA_ROPE:]
    pad = jnp.zeros((w.shape[0], LANES - MLA_ROPE), w.dtype)
    o_ref[...] = jnp.concatenate(nope + qpe + [swa, ckv, kpe, pad], axis=1).astype(o_ref.dtype)


def _permute_w_in(w_in):
    tr = TR_W1
    return pl.pallas_call(
        _w1_kernel,
        grid=(D_MODEL // tr,),
        in_specs=[pl.BlockSpec((tr, w_in.shape[1]), lambda i: (i, 0))],
        out_specs=pl.BlockSpec((tr, N_PROJ), lambda i: (i, 0)),
        out_shape=jax.ShapeDtypeStruct((D_MODEL, N_PROJ), jnp.bfloat16),
        compiler_params=_params(1),
        name="w1prep",
    )(w_in)


def _prep_weights(w_in, w_uk, w_uv):
    bf = jnp.bfloat16
    return dict(
        w1=_permute_w_in(w_in),
        w_uk=w_uk.reshape(KV_RANK, MLA_HEADS * MLA_NOPE).astype(bf),
        w_uvt=w_uv.reshape(KV_RANK, MLA_OUT).T.astype(bf))


def _project(x, w, g_pre_attn, g_ckv, cast=()):
    B, S, _ = x.shape
    return _projection(x.reshape(B * S, D_MODEL), B, S, g_pre_attn, w["w1"], g_ckv, w["w_uk"],
                       w["w_uvt"], *_rope_tables(S), cast=cast)


def _mix_and_ffn(x, proj, late_w, sink, g_out_mla, g_out_swa, g_post_attn, g_pre_ffn, g_post_ffn):
    B, S, _ = x.shape
    q, k, vt, qs, ks, vst = proj
    w_o, w_gate, w_up, w_down = late_w
    oa = _mla_attention(q, k, vt, B, S)
    ob = _swa_attention(sink, qs, ks, vst, B, S)
    x1 = _out_projection(oa, ob, x.reshape(B * S, D_MODEL), g_out_mla, g_out_swa, w_o, g_post_attn)
    return _ffn(x1, g_pre_ffn, w_gate, w_up, w_down, g_post_ffn).reshape(B, S, D_MODEL)


def kernel(x_prompt, x_sample, g_pre_attn, w_in, g_ckv, w_uk, w_uv, sink, g_out_mla, g_out_swa,
           w_o, g_post_attn, g_pre_ffn, w_gate, w_up, w_down, g_post_ffn):
    assert w_in.shape[0] == 1, "single layer"
    w = _prep_weights(w_in[0], w_uk[0], w_uv[0])
    proj_s = _project(x_sample, w, g_pre_attn, g_ckv,
                      cast=(w_o[0], w_gate[0], w_up[0], w_down[0]))
    proj_p = _project(x_prompt, w, g_pre_attn, g_ckv)
    rest = (sink[0], g_out_mla, g_out_swa, g_post_attn, g_pre_ffn, g_post_ffn)
    return (_mix_and_ffn(x_prompt, proj_p, proj_s[6:], *rest),
            _mix_and_ffn(x_sample, proj_s[:6], proj_s[6:], *rest))
```

```python
import functools
import math

import jax
import jax.numpy as jnp
from jax import lax
from jax.experimental import pallas as pl
from jax.experimental.pallas import tpu as pltpu

D_MODEL = 2048
MLA_HEADS = 8
MLA_NOPE = 128
MLA_ROPE = 64
MLA_V = 128
KV_RANK = 512
ROPE_THETA = 10000.0
SWA_HEADS = 8
SWA_KV_HEADS = 2
SWA_HEAD_DIM = 128
SWA_GROUP = SWA_HEADS // SWA_KV_HEADS
WINDOW = 128
BLOCK = 128
D_FF = -(-8 * D_MODEL // (3 * 256)) * 256
EPS = 1e-6

N_MLA_Q = MLA_HEADS * (MLA_NOPE + MLA_ROPE)
N_SWA_Q = SWA_HEADS * SWA_HEAD_DIM
N_SWA_KV = SWA_KV_HEADS * SWA_HEAD_DIM
MLA_OUT = MLA_HEADS * MLA_V
SWA_OUT = SWA_HEADS * SWA_HEAD_DIM

LANES = 128
QK_PAD = 2 * LANES
OFF_QPE = MLA_HEADS * MLA_NOPE
OFF_QS = OFF_QPE + MLA_HEADS * MLA_ROPE
OFF_KS = OFF_QS + N_SWA_Q
OFF_VS = OFF_KS + N_SWA_KV
OFF_CKV = OFF_VS + N_SWA_KV
OFF_KPE = OFF_CKV + KV_RANK
N_PROJ = OFF_KPE + LANES

LOG2E = math.log2(math.e)
MLA_SCALE = (MLA_NOPE + MLA_ROPE) ** -0.5
SWA_SCALE = SWA_HEAD_DIM ** -0.5

VMEM_LIMIT = 56 * 1024 * 1024

TR_W1 = 256
TM_PROJ = 512
TQ_MLA = 256
QT_MLA = 2
BF16_SUBLANES = 16
ONES_ROWS = BF16_SUBLANES
V_ROWS = MLA_V + ONES_ROWS
CK_MLA = 512
TQ_SWA = 2048
TM_OUT = 1024
SUB_OUT = 512
TM_FFN = 1024
TF_FFN = 512
SUB_FFN = 512
SUB_FFN_LAST = 256

_NT = (((1,), (1,)), ((), ()))


def _params(n_axes):
    return pltpu.CompilerParams(
        dimension_semantics=("arbitrary",) * n_axes, vmem_limit_bytes=VMEM_LIMIT)


def _resident(shape):
    return pl.BlockSpec(shape, lambda *_: (0,) * len(shape), pipeline_mode=pl.Buffered(1))


def _rms(x, g):
    var = jnp.mean(x * x, axis=-1, keepdims=True)
    return x * lax.rsqrt(var + EPS) * g


def _rope_pairs(x, cos_t, sin_a, sin_b):
    half = MLA_ROPE // 2
    return x * cos_t + pltpu.roll(x, LANES - half, 1) * sin_a + pltpu.roll(x, half, 1) * sin_b


def _proj_kernel(*refs, n_cast):
    (x_ref, g_ref, w1_ref, gckv_ref, wuk_ref, wuvt_ref, cos_ref, sina_ref, sinb_ref) = refs[:9]
    q_ref, k_ref, vt_ref, qs_ref, ks_ref, vst_ref = refs[9 + n_cast:15 + n_cast]
    for src_ref, dst_ref in zip(refs[9:9 + n_cast], refs[15 + n_cast:]):
        dst_ref[...] = src_ref[...].astype(dst_ref.dtype)

    h = _rms(x_ref[...], g_ref[...]).astype(jnp.bfloat16)
    cos_t, sin_a, sin_b = cos_ref[...], sina_ref[...], sinb_ref[...]
    bf = jnp.bfloat16

    def proj(lo, hi):
        return jnp.dot(h, w1_ref[:, lo:hi], preferred_element_type=jnp.float32)

    low_half = lax.broadcasted_iota(jnp.int32, cos_t.shape, 1) < MLA_ROPE
    q_scale = MLA_SCALE * LOG2E
    q_pe = proj(OFF_QPE, OFF_QS) * q_scale
    for pair in range(MLA_HEADS // 2):
        nope = proj(pair * 2 * MLA_NOPE, (pair + 1) * 2 * MLA_NOPE) * q_scale
        rot = _rope_pairs(q_pe[:, pair * LANES:(pair + 1) * LANES], cos_t, sin_a, sin_b)
        q_ref[2 * pair, :, :LANES] = nope[:, :MLA_NOPE].astype(bf)
        q_ref[2 * pair, :, LANES:] = jnp.where(low_half, rot, 0.0).astype(bf)
        q_ref[2 * pair + 1, :, :LANES] = nope[:, MLA_NOPE:].astype(bf)
        q_ref[2 * pair + 1, :, LANES:] = jnp.where(low_half, pltpu.roll(rot, MLA_ROPE, 1), 0.0).astype(bf)

    qs_ref[...] = (proj(OFF_QS, OFF_KS) * (SWA_SCALE * LOG2E)).astype(bf)
    kvs = proj(OFF_KS, OFF_CKV)
    ks_ref[...] = kvs[:, :N_SWA_KV].astype(bf)
    vst_ref[...] = kvs[:, N_SWA_KV:].T.astype(bf)

    lat = proj(OFF_CKV, N_PROJ)
    c_kv = _rms(lat[:, :KV_RANK], gckv_ref[...]).astype(bf)
    kpe = _rope_pairs(lat[:, KV_RANK:], cos_t, sin_a, sin_b).astype(bf)
    k_nope = jnp.dot(c_kv, wuk_ref[...], preferred_element_type=jnp.float32)
    for hd in range(MLA_HEADS):
        k_ref[hd, :, :LANES] = k_nope[:, hd * MLA_NOPE:(hd + 1) * MLA_NOPE].astype(bf)
        k_ref[hd, :, LANES:] = kpe
    vt = lax.dot_general(wuvt_ref[...], c_kv, _NT, preferred_element_type=jnp.float32)
    for hd in range(MLA_HEADS):
        vt_ref[hd, :MLA_V, :] = vt[hd * MLA_V:(hd + 1) * MLA_V, :].astype(bf)
        vt_ref[hd, MLA_V:, :] = jnp.ones((ONES_ROWS, vt.shape[1]), bf)


def _cast_spec(shape, steps):
    rows, cols = shape
    group = 1
    while (rows * group) % steps or (rows * group // steps) % BF16_SUBLANES:
        group *= 2
    return pl.BlockSpec((rows * group // steps, cols), lambda i: (i // group, 0))


def _projection(x2d, B, S, g_pre, w1, g_ckv, w_uk, w_uvt, cos_t, sin_a, sin_b, cast=()):
    T = B * S
    tm = TM_PROJ
    spt = S // tm
    row = lambda i: (i, 0)
    tab = pl.BlockSpec((tm, LANES), lambda i: (i % spt, 0))
    tposed = lambda n: pl.BlockSpec((None, n, tm), lambda i: (i // spt, 0, i % spt))
    bf = jnp.bfloat16
    cast_specs = [_cast_spec(c.shape, T // tm) for c in cast]
    return pl.pallas_call(
        functools.partial(_proj_kernel, n_cast=len(cast)),
        grid=(T // tm,),
        in_specs=[
            pl.BlockSpec((tm, D_MODEL), row),
            _resident((1, D_MODEL)),
            _resident((D_MODEL, N_PROJ)),
            _resident((1, KV_RANK)),
            _resident((KV_RANK, MLA_HEADS * MLA_NOPE)),
            _resident((MLA_OUT, KV_RANK)),
            tab, tab, tab,
        ] + cast_specs,
        out_specs=[
            pl.BlockSpec((MLA_HEADS, tm, QK_PAD), lambda i: (0, i, 0)),
            pl.BlockSpec((MLA_HEADS, tm, QK_PAD), lambda i: (0, i, 0)),
            pl.BlockSpec((None, MLA_HEADS, V_ROWS, tm), lambda i: (i // spt, 0, 0, i % spt)),
            pl.BlockSpec((tm, N_SWA_Q), row),
            pl.BlockSpec((tm, N_SWA_KV), row),
            tposed(N_SWA_KV),
        ] + cast_specs,
        out_shape=[
            jax.ShapeDtypeStruct((MLA_HEADS, T, QK_PAD), bf),
            jax.ShapeDtypeStruct((MLA_HEADS, T, QK_PAD), bf),
            jax.ShapeDtypeStruct((B, MLA_HEADS, V_ROWS, S), bf),
            jax.ShapeDtypeStruct((T, N_SWA_Q), bf),
            jax.ShapeDtypeStruct((T, N_SWA_KV), bf),
            jax.ShapeDtypeStruct((B, N_SWA_KV, S), bf),
        ] + [jax.ShapeDtypeStruct(c.shape, bf) for c in cast],
        compiler_params=_params(1),
        name="proj",
    )(x2d, g_pre, w1, g_ckv, w_uk, w_uvt, cos_t, sin_a, sin_b, *cast)


def _mla_kernel(q_ref, k_ref, vt_ref, o_ref, sa_ref, sb_ref):
    seq = k_ref.shape[1]
    tq = sa_ref.shape[1]
    n_units = MLA_HEADS * (q_ref.shape[1] // tq)
    chunks = [slice(c * CK_MLA, (c + 1) * CK_MLA) for c in range(seq // CK_MLA)]

    def unit(u):
        hd, qt = u % MLA_HEADS, u // MLA_HEADS
        return hd, slice(qt * tq, (qt + 1) * tq)

    def scores(u, s_ref):
        hd, rows = unit(u)
        q = q_ref[hd, rows, :]
        m = None
        for ck in chunks:
            s = lax.dot_general(k_ref[hd, ck, :], q, _NT, preferred_element_type=jnp.float32)
            s_ref[ck, :] = s
            cm = jnp.max(s, axis=0, keepdims=True)
            m = cm if m is None else jnp.maximum(m, cm)
        return m

    def attend(u, s_ref, m):
        hd, rows = unit(u)
        acc = jnp.zeros((V_ROWS, tq), jnp.float32)
        for ck in chunks:
            p = jnp.exp2(s_ref[ck, :] - m)
            acc = acc + jnp.dot(vt_ref[hd, :, ck], p.astype(jnp.bfloat16),
                                preferred_element_type=jnp.float32)
        o_ref[hd, rows, :] = (acc[:MLA_V] / acc[MLA_V:MLA_V + 1]).T.astype(o_ref.dtype)

    bufs = (sa_ref, sb_ref)
    m_prev = scores(0, bufs[0])
    for u in range(1, n_units):
        m_cur = scores(u, bufs[u % 2])
        attend(u - 1, bufs[(u - 1) % 2], m_prev)
        m_prev = m_cur
    attend(n_units - 1, bufs[(n_units - 1) % 2], m_prev)


def _mla_attention(q, k, vt, B, S):
    T = B * S
    rows = TQ_MLA * QT_MLA
    nq = S // rows
    kv_bytes = S * MLA_HEADS * (QK_PAD + V_ROWS) * 2
    mode = dict(pipeline_mode=pl.Buffered(1)) if 2 * kv_bytes > VMEM_LIMIT // 2 else {}
    return pl.pallas_call(
        _mla_kernel,
        grid=(B, nq),
        in_specs=[
            pl.BlockSpec((MLA_HEADS, rows, QK_PAD), lambda b, i: (0, b * nq + i, 0)),
            pl.BlockSpec((MLA_HEADS, S, QK_PAD), lambda b, i: (0, b, 0), **mode),
            pl.BlockSpec((None, MLA_HEADS, V_ROWS, S), lambda b, i: (b, 0, 0, 0), **mode),
        ],
        out_specs=pl.BlockSpec((MLA_HEADS, rows, MLA_V), lambda b, i: (0, b * nq + i, 0)),
        out_shape=jax.ShapeDtypeStruct((MLA_HEADS, T, MLA_V), jnp.bfloat16),
        scratch_shapes=[pltpu.VMEM((S, TQ_MLA), jnp.float32), pltpu.VMEM((S, TQ_MLA), jnp.float32)],
        compiler_params=_params(2),
        name="mla",
    )(q, k, vt)


def _swa_kernel(sink_ref, bias_ref, q_ref, kp_ref, kc_ref, kn_ref, vp_ref, vc_ref, vn_ref, o_ref,
                sa_ref, sb_ref, *, seq):
    i = pl.program_id(1)
    tq = q_ref.shape[0]
    nblk = tq // BLOCK
    kcat = jnp.concatenate([kp_ref[...], kc_ref[...], kn_ref[...]], axis=0)
    vcat = jnp.concatenate([vp_ref[...], vc_ref[...], vn_ref[...]], axis=1)
    slot = lax.broadcasted_iota(jnp.int32, (3 * BLOCK, 1), 0)
    ones = jnp.ones((ONES_ROWS, 3 * BLOCK), jnp.bfloat16)
    sinks = [jnp.concatenate(
        [jnp.full((1, BLOCK), sink_ref[kv * SWA_GROUP + g] * LOG2E, jnp.float32)
         for g in range(SWA_GROUP)], axis=1) for kv in range(SWA_KV_HEADS)]

    def unit(u):
        kv, jb = u // nblk, u % nblk
        dsl = slice(kv * SWA_HEAD_DIM, (kv + 1) * SWA_HEAD_DIM)
        win = slice(jb * BLOCK, (jb + 3) * BLOCK)
        return kv, jb, dsl, win, slice(jb * BLOCK, (jb + 1) * BLOCK)

    def scores(u, s_ref):
        kv, jb, dsl, win, qrows = unit(u)
        qg = jnp.concatenate(
            [q_ref[qrows, (kv * SWA_GROUP + g) * SWA_HEAD_DIM:(kv * SWA_GROUP + g + 1) * SWA_HEAD_DIM]
             for g in range(SWA_GROUP)], axis=0)
        s = lax.dot_general(kcat[win, dsl], qg, _NT, preferred_element_type=jnp.float32)
        s = s + bias_ref[kv]
        if jb == 0 or jb == nblk - 1:
            key_pos = i * tq + (jb - 1) * BLOCK + slot
            s = jnp.where((key_pos >= 0) & (key_pos < seq), s, -jnp.inf)
        s_ref[...] = s
        return jnp.maximum(jnp.max(s, axis=0, keepdims=True), sinks[kv])

    def attend(u, s_ref, m):
        kv, jb, dsl, win, qrows = unit(u)
        p = jnp.exp2(s_ref[...] - m).astype(jnp.bfloat16)
        acc = jnp.dot(jnp.concatenate([vcat[dsl, win], ones], axis=0), p,
                      preferred_element_type=jnp.float32)
        denom = acc[SWA_HEAD_DIM:SWA_HEAD_DIM + 1] + jnp.exp2(sinks[kv] - m)
        ot = acc[:SWA_HEAD_DIM] / denom
        for g in range(SWA_GROUP):
            hd = kv * SWA_GROUP + g
            o_ref[qrows, hd * SWA_HEAD_DIM:(hd + 1) * SWA_HEAD_DIM] = (
                ot[:, g * BLOCK:(g + 1) * BLOCK].T.astype(o_ref.dtype))

    n_units = SWA_KV_HEADS * nblk
    bufs = (sa_ref, sb_ref)
    m_prev = scores(0, bufs[0])
    for u in range(1, n_units):
        m_cur = scores(u, bufs[u % 2])
        attend(u - 1, bufs[(u - 1) % 2], m_prev)
        m_prev = m_cur
    attend(n_units - 1, bufs[(n_units - 1) % 2], m_prev)


def _swa_bias():
    r = jnp.arange(3 * BLOCK)[:, None]
    qi = jnp.arange(BLOCK)[None, :]
    dist = jnp.abs(qi + BLOCK - r)
    slopes = 2.0 ** (-8.0 * jnp.arange(1, SWA_HEADS + 1, dtype=jnp.float32) / SWA_HEADS)
    b = jnp.where(dist <= WINDOW, -(slopes * LOG2E)[:, None, None] * dist.astype(jnp.float32),
                  -jnp.inf)
    b = b.reshape(SWA_KV_HEADS, SWA_GROUP, 3 * BLOCK, BLOCK).transpose(0, 2, 1, 3)
    return b.reshape(SWA_KV_HEADS, 3 * BLOCK, SWA_GROUP * BLOCK)


def _swa_attention(sink, qs, ks, vst, B, S):
    T = B * S
    tq = TQ_SWA
    nt = S // tq
    per = tq // BLOCK
    nb = S // BLOCK
    prev = lambda j: jnp.maximum(j * per - 1, 0)
    nxt = lambda j: jnp.minimum((j + 1) * per, nb - 1)
    return pl.pallas_call(
        functools.partial(_swa_kernel, seq=S),
        grid=(B, nt),
        in_specs=[
            pl.BlockSpec(memory_space=pltpu.SMEM),
            _resident((SWA_KV_HEADS, 3 * BLOCK, SWA_GROUP * BLOCK)),
            pl.BlockSpec((tq, N_SWA_Q), lambda b, j: (b * nt + j, 0)),
            pl.BlockSpec((BLOCK, N_SWA_KV), lambda b, j: (b * nb + prev(j), 0)),
            pl.BlockSpec((tq, N_SWA_KV), lambda b, j: (b * nt + j, 0)),
            pl.BlockSpec((BLOCK, N_SWA_KV), lambda b, j: (b * nb + nxt(j), 0)),
            pl.BlockSpec((None, N_SWA_KV, BLOCK), lambda b, j: (b, 0, prev(j))),
            pl.BlockSpec((None, N_SWA_KV, tq), lambda b, j: (b, 0, j)),
            pl.BlockSpec((None, N_SWA_KV, BLOCK), lambda b, j: (b, 0, nxt(j))),
        ],
        out_specs=pl.BlockSpec((tq, SWA_OUT), lambda b, j: (b * nt + j, 0)),
        out_shape=jax.ShapeDtypeStruct((T, SWA_OUT), jnp.bfloat16),
        scratch_shapes=[pltpu.VMEM((3 * BLOCK, SWA_GROUP * BLOCK), jnp.float32)] * 2,
        compiler_params=_params(2),
        name="swa",
    )(sink, _swa_bias(), qs, ks, ks, ks, vst, vst, vst)


def _out_kernel(oa_ref, ob_ref, x_ref, ga_ref, gb_ref, wo_ref, gpost_ref, x1_ref):
    for r0 in range(0, x_ref.shape[0], SUB_OUT):
        rows = slice(r0, r0 + SUB_OUT)
        oa = jnp.concatenate([oa_ref[hd, rows, :] for hd in range(MLA_HEADS)], axis=1)
        na = _rms(oa.astype(jnp.float32), ga_ref[...]).astype(jnp.bfloat16)
        nb = _rms(ob_ref[rows, :].astype(jnp.float32), gb_ref[...]).astype(jnp.bfloat16)
        o = jnp.dot(jnp.concatenate([na, nb], axis=1), wo_ref[...],
                    preferred_element_type=jnp.float32)
        x1_ref[rows, :] = x_ref[rows, :] + _rms(o, gpost_ref[...])


def _out_projection(oa, ob, x2d, g_a, g_b, w_o, g_post):
    T = x2d.shape[0]
    tm = TM_OUT
    row = lambda i: (i, 0)
    return pl.pallas_call(
        _out_kernel,
        grid=(T // tm,),
        in_specs=[
            pl.BlockSpec((MLA_HEADS, tm, MLA_V), lambda i: (0, i, 0)),
            pl.BlockSpec((tm, SWA_OUT), row),
            pl.BlockSpec((tm, D_MODEL), row),
            _resident((1, MLA_OUT)),
            _resident((1, SWA_OUT)),
            _resident((MLA_OUT + SWA_OUT, D_MODEL)),
            _resident((1, D_MODEL)),
        ],
        out_specs=pl.BlockSpec((tm, D_MODEL), row),
        out_shape=jax.ShapeDtypeStruct((T, D_MODEL), jnp.float32),
        compiler_params=_params(1),
        name="outproj",
    )(oa, ob, x2d, g_a, g_b, w_o, g_post)


def _ffn_kernel(x1_ref, gpre_ref, wg_ref, wu_ref, wd_ref, gpost_ref, y_ref, h_ref):
    j = pl.program_id(1)
    last_j = pl.num_programs(1) - 1

    def step(first, last):
        sub = h_ref.shape[0]
        if first:
            sub = SUB_FFN
        if last:
            sub = SUB_FFN_LAST
        for r0 in range(0, h_ref.shape[0], sub):
            rows = slice(r0, r0 + sub)
            if first:
                h = _rms(x1_ref[rows, :], gpre_ref[...]).astype(h_ref.dtype)
                h_ref[rows, :] = h
            else:
                h = h_ref[rows, :]
            gate = jnp.dot(h, wg_ref[...], preferred_element_type=jnp.float32)
            up = jnp.dot(h, wu_ref[...], preferred_element_type=jnp.float32)
            act = (gate * jax.nn.sigmoid(gate) * up).astype(jnp.bfloat16)
            if first or last:
                acc = jnp.dot(act, wd_ref[...], preferred_element_type=jnp.float32)
                if not first:
                    acc = y_ref[rows, :] + acc
                if last:
                    acc = x1_ref[rows, :] + _rms(acc, gpost_ref[...])
                y_ref[rows, :] = acc
            else:
                for c0 in range(0, D_MODEL, D_MODEL // 2):
                    cols = slice(c0, c0 + D_MODEL // 2)
                    y_ref[rows, cols] += jnp.dot(act, wd_ref[:, cols],
                                                 preferred_element_type=jnp.float32)

    pl.when(j == 0)(functools.partial(step, True, False))
    pl.when((j > 0) & (j < last_j))(functools.partial(step, False, False))
    pl.when(j == last_j)(functools.partial(step, False, True))


def _ffn(x1, g_pre, w_gate, w_up, w_down, g_post):
    T = x1.shape[0]
    tm, tf = TM_FFN, TF_FFN
    assert D_FF // tf >= 2, "first and last d_ff chunk must be different steps"
    row = lambda i, j: (i, 0)
    return pl.pallas_call(
        _ffn_kernel,
        grid=(T // tm, D_FF // tf),
        in_specs=[
            pl.BlockSpec((tm, D_MODEL), row),
            _resident((1, D_MODEL)),
            pl.BlockSpec((D_MODEL, tf), lambda i, j: (0, j)),
            pl.BlockSpec((D_MODEL, tf), lambda i, j: (0, j)),
            pl.BlockSpec((tf, D_MODEL), lambda i, j: (j, 0)),
            _resident((1, D_MODEL)),
        ],
        out_specs=pl.BlockSpec((tm, D_MODEL), row),
        out_shape=jax.ShapeDtypeStruct((T, D_MODEL), jnp.float32),
        scratch_shapes=[pltpu.VMEM((tm, D_MODEL), jnp.bfloat16)],
        compiler_params=_params(2),
        name="ffn",
    )(x1, g_pre, w_gate, w_up, w_down, g_post)


def _rope_tables(S):
    inv = ROPE_THETA ** (-jnp.arange(0, MLA_ROPE, 2, dtype=jnp.float32) / MLA_ROPE)
    ang = jnp.arange(S, dtype=jnp.float32)[:, None] * inv[None, :]
    c, s = jnp.cos(ang), jnp.sin(ang)
    z = jnp.zeros_like(c)
    cos_t = jnp.concatenate([c, c, c, c], axis=1)
    sin_a = jnp.concatenate([-s, z, -s, z], axis=1)
    sin_b = jnp.concatenate([z, s, z, s], axis=1)
    return cos_t, sin_a, sin_b


def _w1_kernel(w_ref, o_ref):
    w = w_ref[...]
    hw = MLA_NOPE + MLA_ROPE
    nope = [w[:, hd * hw:hd * hw + MLA_NOPE] for hd in range(MLA_HEADS)]
    qpe = [w[:, hd * hw + MLA_NOPE:(hd + 1) * hw] for hd in range(MLA_HEADS)]
    o = N_MLA_Q
    ckv = w[:, o:o + KV_RANK]
    kpe = w[:, o + KV_RANK:o + KV_RANK + MLA_ROPE]
    swa = w[:, o + KV_RANK + MLA_ROPE:]
    pad = jnp.zeros((w.shape[0], LANES - MLA_ROPE), w.dtype)
    o_ref[...] = jnp.concatenate(nope + qpe + [swa, ckv, kpe, pad], axis=1).astype(o_ref.dtype)


def _permute_w_in(w_in):
    tr = TR_W1
    return pl.pallas_call(
        _w1_kernel,
        grid=(D_MODEL // tr,),
        in_specs=[pl.BlockSpec((tr, w_in.shape[1]), lambda i: (i, 0))],
        out_specs=pl.BlockSpec((tr, N_PROJ), lambda i: (i, 0)),
        out_shape=jax.ShapeDtypeStruct((D_MODEL, N_PROJ), jnp.bfloat16),
        compiler_params=_params(1),
        name="w1prep",
    )(w_in)


def _prep_weights(w_in, w_uk, w_uv):
    bf = jnp.bfloat16
    return dict(
        w1=_permute_w_in(w_in),
        w_uk=w_uk.reshape(KV_RANK, MLA_HEADS * MLA_NOPE).astype(bf),
        w_uvt=w_uv.reshape(KV_RANK, MLA_OUT).T.astype(bf))


def _project(x, w, g_pre_attn, g_ckv, cast=()):
    B, S, _ = x.shape
    return _projection(x.reshape(B * S, D_MODEL), B, S, g_pre_attn, w["w1"], g_ckv, w["w_uk"],
                       w["w_uvt"], *_rope_tables(S), cast=cast)


def _mix_and_ffn(x, proj, late_w, sink, g_out_mla, g_out_swa, g_post_attn, g_pre_ffn, g_post_ffn):
    B, S, _ = x.shape
    q, k, vt, qs, ks, vst = proj
    w_o, w_gate, w_up, w_down = late_w
    oa = _mla_attention(q, k, vt, B, S)
    ob = _swa_attention(sink, qs, ks, vst, B, S)
    x1 = _out_projection(oa, ob, x.reshape(B * S, D_MODEL), g_out_mla, g_out_swa, w_o, g_post_attn)
    return _ffn(x1, g_pre_ffn, w_gate, w_up, w_down, g_post_ffn).reshape(B, S, D_MODEL)


def kernel(x_prompt, x_sample, g_pre_attn, w_in, g_ckv, w_uk, w_uv, sink, g_out_mla, g_out_swa,
           w_o, g_post_attn, g_pre_ffn, w_gate, w_up, w_down, g_post_ffn):
    assert w_in.shape[0] == 1, "single layer"
    w = _prep_weights(w_in[0], w_uk[0], w_uv[0])
    proj_s = _project(x_sample, w, g_pre_attn, g_ckv,
                      cast=(w_o[0], w_gate[0], w_up[0], w_down[0]))
    proj_p = _project(x_prompt, w, g_pre_attn, g_ckv)
    rest = (sink[0], g_out_mla, g_out_swa, g_post_attn, g_pre_ffn, g_post_ffn)
    return (_mix_and_ffn(x_prompt, proj_p, proj_s[6:], *rest),
            _mix_and_ffn(x_sample, proj_s[:6], proj_s[6:], *rest))
```

```python
import functools
import math

import jax
import jax.numpy as jnp
from jax import lax
from jax.experimental import pallas as pl
from jax.experimental.pallas import tpu as pltpu

D_MODEL = 2048
MLA_HEADS = 8
MLA_NOPE = 128
MLA_ROPE = 64
MLA_V = 128
KV_RANK = 512
ROPE_THETA = 10000.0
SWA_HEADS = 8
SWA_KV_HEADS = 2
SWA_HEAD_DIM = 128
SWA_GROUP = SWA_HEADS // SWA_KV_HEADS
WINDOW = 128
BLOCK = 128
D_FF = -(-8 * D_MODEL // (3 * 256)) * 256
EPS = 1e-6

N_MLA_Q = MLA_HEADS * (MLA_NOPE + MLA_ROPE)
N_SWA_Q = SWA_HEADS * SWA_HEAD_DIM
N_SWA_KV = SWA_KV_HEADS * SWA_HEAD_DIM
MLA_OUT = MLA_HEADS * MLA_V
SWA_OUT = SWA_HEADS * SWA_HEAD_DIM

LANES = 128
QK_PAD = 2 * LANES
OFF_QPE = MLA_HEADS * MLA_NOPE
OFF_QS = OFF_QPE + MLA_HEADS * MLA_ROPE
OFF_KS = OFF_QS + N_SWA_Q
OFF_VS = OFF_KS + N_SWA_KV
OFF_CKV = OFF_VS + N_SWA_KV
OFF_KPE = OFF_CKV + KV_RANK
N_PROJ = OFF_KPE + LANES

LOG2E = math.log2(math.e)
MLA_SCALE = (MLA_NOPE + MLA_ROPE) ** -0.5
SWA_SCALE = SWA_HEAD_DIM ** -0.5

VMEM_LIMIT = 56 * 1024 * 1024

TR_W1 = 256
TM_PROJ = 512
TQ_MLA = 256
QT_MLA = 2
BF16_SUBLANES = 16
ONES_ROWS = BF16_SUBLANES
V_ROWS = MLA_V + ONES_ROWS
CK_MLA = 512
TQ_SWA = 2048
TM_OUT = 512
TM_FFN = 1024
TF_FFN = 512
SUB_FFN = 512

_NT = (((1,), (1,)), ((), ()))


def _params(n_axes):
    return pltpu.CompilerParams(
        dimension_semantics=("arbitrary",) * n_axes, vmem_limit_bytes=VMEM_LIMIT)


def _resident(shape):
    return pl.BlockSpec(shape, lambda *_: (0,) * len(shape), pipeline_mode=pl.Buffered(1))


def _rms(x, g):
    var = jnp.mean(x * x, axis=-1, keepdims=True)
    return x * lax.rsqrt(var + EPS) * g


def _rope_pairs(x, cos_t, sin_a, sin_b):
    half = MLA_ROPE // 2
    return x * cos_t + pltpu.roll(x, LANES - half, 1) * sin_a + pltpu.roll(x, half, 1) * sin_b


def _proj_kernel(*refs, n_cast):
    (x_ref, g_ref, w1_ref, gckv_ref, wuk_ref, wuvt_ref, cos_ref, sina_ref, sinb_ref) = refs[:9]
    q_ref, k_ref, vt_ref, qs_ref, ks_ref, vst_ref = refs[9 + n_cast:15 + n_cast]
    for src_ref, dst_ref in zip(refs[9:9 + n_cast], refs[15 + n_cast:]):
        dst_ref[...] = src_ref[...].astype(dst_ref.dtype)

    h = _rms(x_ref[...], g_ref[...]).astype(jnp.bfloat16)
    cos_t, sin_a, sin_b = cos_ref[...], sina_ref[...], sinb_ref[...]
    bf = jnp.bfloat16

    def proj(lo, hi):
        return jnp.dot(h, w1_ref[:, lo:hi], preferred_element_type=jnp.float32)

    low_half = lax.broadcasted_iota(jnp.int32, cos_t.shape, 1) < MLA_ROPE
    q_scale = MLA_SCALE * LOG2E
    q_pe = proj(OFF_QPE, OFF_QS) * q_scale
    for pair in range(MLA_HEADS // 2):
        nope = proj(pair * 2 * MLA_NOPE, (pair + 1) * 2 * MLA_NOPE) * q_scale
        rot = _rope_pairs(q_pe[:, pair * LANES:(pair + 1) * LANES], cos_t, sin_a, sin_b)
        q_ref[2 * pair, :, :LANES] = nope[:, :MLA_NOPE].astype(bf)
        q_ref[2 * pair, :, LANES:] = jnp.where(low_half, rot, 0.0).astype(bf)
        q_ref[2 * pair + 1, :, :LANES] = nope[:, MLA_NOPE:].astype(bf)
        q_ref[2 * pair + 1, :, LANES:] = jnp.where(low_half, pltpu.roll(rot, MLA_ROPE, 1), 0.0).astype(bf)

    qs_ref[...] = (proj(OFF_QS, OFF_KS) * (SWA_SCALE * LOG2E)).astype(bf)
    kvs = proj(OFF_KS, OFF_CKV)
    ks_ref[...] = kvs[:, :N_SWA_KV].astype(bf)
    vst_ref[...] = kvs[:, N_SWA_KV:].T.astype(bf)

    lat = proj(OFF_CKV, N_PROJ)
    c_kv = _rms(lat[:, :KV_RANK], gckv_ref[...]).astype(bf)
    kpe = _rope_pairs(lat[:, KV_RANK:], cos_t, sin_a, sin_b).astype(bf)
    k_nope = jnp.dot(c_kv, wuk_ref[...], preferred_element_type=jnp.float32)
    for hd in range(MLA_HEADS):
        k_ref[hd, :, :LANES] = k_nope[:, hd * MLA_NOPE:(hd + 1) * MLA_NOPE].astype(bf)
        k_ref[hd, :, LANES:] = kpe
    vt = lax.dot_general(wuvt_ref[...], c_kv, _NT, preferred_element_type=jnp.float32)
    for hd in range(MLA_HEADS):
        vt_ref[hd, :MLA_V, :] = vt[hd * MLA_V:(hd + 1) * MLA_V, :].astype(bf)
        vt_ref[hd, MLA_V:, :] = jnp.ones((ONES_ROWS, vt.shape[1]), bf)


def _cast_spec(shape, steps):
    rows, cols = shape
    group = 1
    while (rows * group) % steps or (rows * group // steps) % BF16_SUBLANES:
        group *= 2
    return pl.BlockSpec((rows * group // steps, cols), lambda i: (i // group, 0))


def _projection(x2d, B, S, g_pre, w1, g_ckv, w_uk, w_uvt, cos_t, sin_a, sin_b, cast=()):
    T = B * S
    tm = TM_PROJ
    spt = S // tm
    row = lambda i: (i, 0)
    tab = pl.BlockSpec((tm, LANES), lambda i: (i % spt, 0))
    tposed = lambda n: pl.BlockSpec((None, n, tm), lambda i: (i // spt, 0, i % spt))
    bf = jnp.bfloat16
    cast_specs = [_cast_spec(c.shape, T // tm) for c in cast]
    return pl.pallas_call(
        functools.partial(_proj_kernel, n_cast=len(cast)),
        grid=(T // tm,),
        in_specs=[
            pl.BlockSpec((tm, D_MODEL), row),
            _resident((1, D_MODEL)),
            _resident((D_MODEL, N_PROJ)),
            _resident((1, KV_RANK)),
            _resident((KV_RANK, MLA_HEADS * MLA_NOPE)),
            _resident((MLA_OUT, KV_RANK)),
            tab, tab, tab,
        ] + cast_specs,
        out_specs=[
            pl.BlockSpec((MLA_HEADS, tm, QK_PAD), lambda i: (0, i, 0)),
            pl.BlockSpec((MLA_HEADS, tm, QK_PAD), lambda i: (0, i, 0)),
            pl.BlockSpec((None, MLA_HEADS, V_ROWS, tm), lambda i: (i // spt, 0, 0, i % spt)),
            pl.BlockSpec((tm, N_SWA_Q), row),
            pl.BlockSpec((tm, N_SWA_KV), row),
            tposed(N_SWA_KV),
        ] + cast_specs,
        out_shape=[
            jax.ShapeDtypeStruct((MLA_HEADS, T, QK_PAD), bf),
            jax.ShapeDtypeStruct((MLA_HEADS, T, QK_PAD), bf),
            jax.ShapeDtypeStruct((B, MLA_HEADS, V_ROWS, S), bf),
            jax.ShapeDtypeStruct((T, N_SWA_Q), bf),
            jax.ShapeDtypeStruct((T, N_SWA_KV), bf),
            jax.ShapeDtypeStruct((B, N_SWA_KV, S), bf),
        ] + [jax.ShapeDtypeStruct(c.shape, bf) for c in cast],
        compiler_params=_params(1),
        name="proj",
    )(x2d, g_pre, w1, g_ckv, w_uk, w_uvt, cos_t, sin_a, sin_b, *cast)


def _mla_kernel(q_ref, k_ref, vt_ref, o_ref, sa_ref, sb_ref):
    seq = k_ref.shape[1]
    tq = sa_ref.shape[1]
    n_units = MLA_HEADS * (q_ref.shape[1] // tq)
    chunks = [slice(c * CK_MLA, (c + 1) * CK_MLA) for c in range(seq // CK_MLA)]

    def unit(u):
        hd, qt = u % MLA_HEADS, u // MLA_HEADS
        return hd, slice(qt * tq, (qt + 1) * tq)

    def scores(u, s_ref):
        hd, rows = unit(u)
        q = q_ref[hd, rows, :]
        m = None
        for ck in chunks:
            s = lax.dot_general(k_ref[hd, ck, :], q, _NT, preferred_element_type=jnp.float32)
            s_ref[ck, :] = s
            cm = jnp.max(s, axis=0, keepdims=True)
            m = cm if m is None else jnp.maximum(m, cm)
        return m

    def attend(u, s_ref, m):
        hd, rows = unit(u)
        acc = jnp.zeros((V_ROWS, tq), jnp.float32)
        for ck in chunks:
            p = jnp.exp2(s_ref[ck, :] - m)
            acc = acc + jnp.dot(vt_ref[hd, :, ck], p.astype(jnp.bfloat16),
                                preferred_element_type=jnp.float32)
        o_ref[hd, rows, :] = (acc[:MLA_V] / acc[MLA_V:MLA_V + 1]).T.astype(o_ref.dtype)

    bufs = (sa_ref, sb_ref)
    m_prev = scores(0, bufs[0])
    for u in range(1, n_units):
        m_cur = scores(u, bufs[u % 2])
        attend(u - 1, bufs[(u - 1) % 2], m_prev)
        m_prev = m_cur
    attend(n_units - 1, bufs[(n_units - 1) % 2], m_prev)


def _mla_attention(q, k, vt, B, S):
    T = B * S
    rows = TQ_MLA * QT_MLA
    nq = S // rows
    kv_bytes = S * MLA_HEADS * (QK_PAD + V_ROWS) * 2
    mode = dict(pipeline_mode=pl.Buffered(1)) if 2 * kv_bytes > VMEM_LIMIT // 2 else {}
    return pl.pallas_call(
        _mla_kernel,
        grid=(B, nq),
        in_specs=[
            pl.BlockSpec((MLA_HEADS, rows, QK_PAD), lambda b, i: (0, b * nq + i, 0)),
            pl.BlockSpec((MLA_HEADS, S, QK_PAD), lambda b, i: (0, b, 0), **mode),
            pl.BlockSpec((None, MLA_HEADS, V_ROWS, S), lambda b, i: (b, 0, 0, 0), **mode),
        ],
        out_specs=pl.BlockSpec((MLA_HEADS, rows, MLA_V), lambda b, i: (0, b * nq + i, 0)),
        out_shape=jax.ShapeDtypeStruct((MLA_HEADS, T, MLA_V), jnp.bfloat16),
        scratch_shapes=[pltpu.VMEM((S, TQ_MLA), jnp.float32), pltpu.VMEM((S, TQ_MLA), jnp.float32)],
        compiler_params=_params(2),
        name="mla",
    )(q, k, vt)


def _swa_kernel(sink_ref, bias_ref, q_ref, kp_ref, kc_ref, kn_ref, vp_ref, vc_ref, vn_ref, o_ref,
                sa_ref, sb_ref, *, seq):
    i = pl.program_id(1)
    tq = q_ref.shape[0]
    nblk = tq // BLOCK
    kcat = jnp.concatenate([kp_ref[...], kc_ref[...], kn_ref[...]], axis=0)
    vcat = jnp.concatenate([vp_ref[...], vc_ref[...], vn_ref[...]], axis=1)
    slot = lax.broadcasted_iota(jnp.int32, (3 * BLOCK, 1), 0)
    ones = jnp.ones((ONES_ROWS, 3 * BLOCK), jnp.bfloat16)
    sinks = [jnp.concatenate(
        [jnp.full((1, BLOCK), sink_ref[kv * SWA_GROUP + g] * LOG2E, jnp.float32)
         for g in range(SWA_GROUP)], axis=1) for kv in range(SWA_KV_HEADS)]

    def unit(u):
        kv, jb = u // nblk, u % nblk
        dsl = slice(kv * SWA_HEAD_DIM, (kv + 1) * SWA_HEAD_DIM)
        win = slice(jb * BLOCK, (jb + 3) * BLOCK)
        return kv, jb, dsl, win, slice(jb * BLOCK, (jb + 1) * BLOCK)

    def scores(u, s_ref):
        kv, jb, dsl, win, qrows = unit(u)
        qg = jnp.concatenate(
            [q_ref[qrows, (kv * SWA_GROUP + g) * SWA_HEAD_DIM:(kv * SWA_GROUP + g + 1) * SWA_HEAD_DIM]
             for g in range(SWA_GROUP)], axis=0)
        s = lax.dot_general(kcat[win, dsl], qg, _NT, preferred_element_type=jnp.float32)
        s = s + bias_ref[kv]
        if jb == 0 or jb == nblk - 1:
            key_pos = i * tq + (jb - 1) * BLOCK + slot
            s = jnp.where((key_pos >= 0) & (key_pos < seq), s, -jnp.inf)
        s_ref[...] = s
        return jnp.maximum(jnp.max(s, axis=0, keepdims=True), sinks[kv])

    def attend(u, s_ref, m):
        kv, jb, dsl, win, qrows = unit(u)
        p = jnp.exp2(s_ref[...] - m).astype(jnp.bfloat16)
        acc = jnp.dot(jnp.concatenate([vcat[dsl, win], ones], axis=0), p,
                      preferred_element_type=jnp.float32)
        denom = acc[SWA_HEAD_DIM:SWA_HEAD_DIM + 1] + jnp.exp2(sinks[kv] - m)
        ot = acc[:SWA_HEAD_DIM] / denom
        for g in range(SWA_GROUP):
            hd = kv * SWA_GROUP + g
            o_ref[qrows, hd * SWA_HEAD_DIM:(hd + 1) * SWA_HEAD_DIM] = (
                ot[:, g * BLOCK:(g + 1) * BLOCK].T.astype(o_ref.dtype))

    n_units = SWA_KV_HEADS * nblk
    bufs = (sa_ref, sb_ref)
    m_prev = scores(0, bufs[0])
    for u in range(1, n_units):
        m_cur = scores(u, bufs[u % 2])
        attend(u - 1, bufs[(u - 1) % 2], m_prev)
        m_prev = m_cur
    attend(n_units - 1, bufs[(n_units - 1) % 2], m_prev)


def _swa_bias():
    r = jnp.arange(3 * BLOCK)[:, None]
    qi = jnp.arange(BLOCK)[None, :]
    dist = jnp.abs(qi + BLOCK - r)
    slopes = 2.0 ** (-8.0 * jnp.arange(1, SWA_HEADS + 1, dtype=jnp.float32) / SWA_HEADS)
    b = jnp.where(dist <= WINDOW, -(slopes * LOG2E)[:, None, None] * dist.astype(jnp.float32),
                  -jnp.inf)
    b = b.reshape(SWA_KV_HEADS, SWA_GROUP, 3 * BLOCK, BLOCK).transpose(0, 2, 1, 3)
    return b.reshape(SWA_KV_HEADS, 3 * BLOCK, SWA_GROUP * BLOCK)


def _swa_attention(sink, qs, ks, vst, B, S):
    T = B * S
    tq = TQ_SWA
    nt = S // tq
    per = tq // BLOCK
    nb = S // BLOCK
    prev = lambda j: jnp.maximum(j * per - 1, 0)
    nxt = lambda j: jnp.minimum((j + 1) * per, nb - 1)
    return pl.pallas_call(
        functools.partial(_swa_kernel, seq=S),
        grid=(B, nt),
        in_specs=[
            pl.BlockSpec(memory_space=pltpu.SMEM),
            _resident((SWA_KV_HEADS, 3 * BLOCK, SWA_GROUP * BLOCK)),
            pl.BlockSpec((tq, N_SWA_Q), lambda b, j: (b * nt + j, 0)),
            pl.BlockSpec((BLOCK, N_SWA_KV), lambda b, j: (b * nb + prev(j), 0)),
            pl.BlockSpec((tq, N_SWA_KV), lambda b, j: (b * nt + j, 0)),
            pl.BlockSpec((BLOCK, N_SWA_KV), lambda b, j: (b * nb + nxt(j), 0)),
            pl.BlockSpec((None, N_SWA_KV, BLOCK), lambda b, j: (b, 0, prev(j))),
            pl.BlockSpec((None, N_SWA_KV, tq), lambda b, j: (b, 0, j)),
            pl.BlockSpec((None, N_SWA_KV, BLOCK), lambda b, j: (b, 0, nxt(j))),
        ],
        out_specs=pl.BlockSpec((tq, SWA_OUT), lambda b, j: (b * nt + j, 0)),
        out_shape=jax.ShapeDtypeStruct((T, SWA_OUT), jnp.bfloat16),
        scratch_shapes=[pltpu.VMEM((3 * BLOCK, SWA_GROUP * BLOCK), jnp.float32)] * 2,
        compiler_params=_params(2),
        name="swa",
    )(sink, _swa_bias(), qs, ks, ks, ks, vst, vst, vst)


def _out_kernel(oa_ref, ob_ref, x_ref, ga_ref, gb_ref, wo_ref, gpost_ref, x1_ref):
    oa = jnp.concatenate([oa_ref[hd] for hd in range(MLA_HEADS)], axis=1)
    na = _rms(oa.astype(jnp.float32), ga_ref[...]).astype(jnp.bfloat16)
    nb = _rms(ob_ref[...].astype(jnp.float32), gb_ref[...]).astype(jnp.bfloat16)
    o = jnp.dot(jnp.concatenate([na, nb], axis=1), wo_ref[...],
                preferred_element_type=jnp.float32)
    x1_ref[...] = x_ref[...] + _rms(o, gpost_ref[...])


def _out_projection(oa, ob, x2d, g_a, g_b, w_o, g_post):
    T = x2d.shape[0]
    tm = TM_OUT
    row = lambda i: (i, 0)
    return pl.pallas_call(
        _out_kernel,
        grid=(T // tm,),
        in_specs=[
            pl.BlockSpec((MLA_HEADS, tm, MLA_V), lambda i: (0, i, 0)),
            pl.BlockSpec((tm, SWA_OUT), row),
            pl.BlockSpec((tm, D_MODEL), row),
            _resident((1, MLA_OUT)),
            _resident((1, SWA_OUT)),
            _resident((MLA_OUT + SWA_OUT, D_MODEL)),
            _resident((1, D_MODEL)),
        ],
        out_specs=pl.BlockSpec((tm, D_MODEL), row),
        out_shape=jax.ShapeDtypeStruct((T, D_MODEL), jnp.float32),
        compiler_params=_params(1),
        name="outproj",
    )(oa, ob, x2d, g_a, g_b, w_o, g_post)


def _ffn_kernel(x1_ref, gpre_ref, wg_ref, wu_ref, wd_ref, gpost_ref, y_ref, h_ref):
    j = pl.program_id(1)
    last_j = pl.num_programs(1) - 1

    def step(first, last):
        sub = SUB_FFN if (first or last) else h_ref.shape[0]
        for r0 in range(0, h_ref.shape[0], sub):
            rows = slice(r0, r0 + sub)
            if first:
                h = _rms(x1_ref[rows, :], gpre_ref[...]).astype(h_ref.dtype)
                h_ref[rows, :] = h
            else:
                h = h_ref[rows, :]
            gate = jnp.dot(h, wg_ref[...], preferred_element_type=jnp.float32)
            up = jnp.dot(h, wu_ref[...], preferred_element_type=jnp.float32)
            act = (gate * jax.nn.sigmoid(gate) * up).astype(jnp.bfloat16)
            if first or last:
                acc = jnp.dot(act, wd_ref[...], preferred_element_type=jnp.float32)
                if not first:
                    acc = y_ref[rows, :] + acc
                if last:
                    acc = x1_ref[rows, :] + _rms(acc, gpost_ref[...])
                y_ref[rows, :] = acc
            else:
                for c0 in range(0, D_MODEL, D_MODEL // 2):
                    cols = slice(c0, c0 + D_MODEL // 2)
                    y_ref[rows, cols] += jnp.dot(act, wd_ref[:, cols],
                                                 preferred_element_type=jnp.float32)

    pl.when(j == 0)(functools.partial(step, True, False))
    pl.when((j > 0) & (j < last_j))(functools.partial(step, False, False))
    pl.when(j == last_j)(functools.partial(step, False, True))


def _ffn(x1, g_pre, w_gate, w_up, w_down, g_post):
    T = x1.shape[0]
    tm, tf = TM_FFN, TF_FFN
    assert D_FF // tf >= 2, "first and last d_ff chunk must be different steps"
    row = lambda i, j: (i, 0)
    return pl.pallas_call(
        _ffn_kernel,
        grid=(T // tm, D_FF // tf),
        in_specs=[
            pl.BlockSpec((tm, D_MODEL), row),
            _resident((1, D_MODEL)),
            pl.BlockSpec((D_MODEL, tf), lambda i, j: (0, j)),
            pl.BlockSpec((D_MODEL, tf), lambda i, j: (0, j)),
            pl.BlockSpec((tf, D_MODEL), lambda i, j: (j, 0)),
            _resident((1, D_MODEL)),
        ],
        out_specs=pl.BlockSpec((tm, D_MODEL), row),
        out_shape=jax.ShapeDtypeStruct((T, D_MODEL), jnp.float32),
        scratch_shapes=[pltpu.VMEM((tm, D_MODEL), jnp.bfloat16)],
        compiler_params=_params(2),
        name="ffn",
    )(x1, g_pre, w_gate, w_up, w_down, g_post)


def _rope_tables(S):
    inv = ROPE_THETA ** (-jnp.arange(0, MLA_ROPE, 2, dtype=jnp.float32) / MLA_ROPE)
    ang = jnp.arange(S, dtype=jnp.float32)[:, None] * inv[None, :]
    c, s = jnp.cos(ang), jnp.sin(ang)
    z = jnp.zeros_like(c)
    cos_t = jnp.concatenate([c, c, c, c], axis=1)
    sin_a = jnp.concatenate([-s, z, -s, z], axis=1)
    sin_b = jnp.concatenate([z, s, z, s], axis=1)
    return cos_t, sin_a, sin_b


def _w1_kernel(w_ref, o_ref):
    w = w_ref[...]
    hw = MLA_NOPE + MLA_ROPE
    nope = [w[:, hd * hw:hd * hw + MLA_NOPE] for hd in range(MLA_HEADS)]
    qpe = [w[:, hd * hw + MLA_NOPE:(hd + 1) * hw] for hd in range(MLA_HEADS)]
    o = N_MLA_Q
    ckv = w[:, o:o + KV_RANK]
    kpe = w[:, o + KV_RANK:o + KV_RANK + MLA_ROPE]
    swa = w[:, o + KV_RANK + MLA_ROPE:]
    pad = jnp.zeros((w.shape[0], LANES - MLA_ROPE), w.dtype)
    o_ref[...] = jnp.concatenate(nope + qpe + [swa, ckv, kpe, pad], axis=1).astype(o_ref.dtype)


def _permute_w_in(w_in):
    tr = TR_W1
    return pl.pallas_call(
        _w1_kernel,
        grid=(D_MODEL // tr,),
        in_specs=[pl.BlockSpec((tr, w_in.shape[1]), lambda i: (i, 0))],
        out_specs=pl.BlockSpec((tr, N_PROJ), lambda i: (i, 0)),
        out_shape=jax.ShapeDtypeStruct((D_MODEL, N_PROJ), jnp.bfloat16),
        compiler_params=_params(1),
        name="w1prep",
    )(w_in)


def _prep_weights(w_in, w_uk, w_uv):
    bf = jnp.bfloat16
    return dict(
        w1=_permute_w_in(w_in),
        w_uk=w_uk.reshape(KV_RANK, MLA_HEADS * MLA_NOPE).astype(bf),
        w_uvt=w_uv.reshape(KV_RANK, MLA_OUT).T.astype(bf))


def _project(x, w, g_pre_attn, g_ckv, cast=()):
    B, S, _ = x.shape
    return _projection(x.reshape(B * S, D_MODEL), B, S, g_pre_attn, w["w1"], g_ckv, w["w_uk"],
                       w["w_uvt"], *_rope_tables(S), cast=cast)


def _mix_and_ffn(x, proj, late_w, sink, g_out_mla, g_out_swa, g_post_attn, g_pre_ffn, g_post_ffn):
    B, S, _ = x.shape
    q, k, vt, qs, ks, vst = proj
    w_o, w_gate, w_up, w_down = late_w
    oa = _mla_attention(q, k, vt, B, S)
    ob = _swa_attention(sink, qs, ks, vst, B, S)
    x1 = _out_projection(oa, ob, x.reshape(B * S, D_MODEL), g_out_mla, g_out_swa, w_o, g_post_attn)
    return _ffn(x1, g_pre_ffn, w_gate, w_up, w_down, g_post_ffn).reshape(B, S, D_MODEL)


def kernel(x_prompt, x_sample, g_pre_attn, w_in, g_ckv, w_uk, w_uv, sink, g_out_mla, g_out_swa,
           w_o, g_post_attn, g_pre_ffn, w_gate, w_up, w_down, g_post_ffn):
    assert w_in.shape[0] == 1, "single layer"
    w = _prep_weights(w_in[0], w_uk[0], w_uv[0])
    proj_s = _project(x_sample, w, g_pre_attn, g_ckv,
                      cast=(w_o[0], w_gate[0], w_up[0], w_down[0]))
    proj_p = _project(x_prompt, w, g_pre_attn, g_ckv)
    rest = (sink[0], g_out_mla, g_out_swa, g_post_attn, g_pre_ffn, g_post_ffn)
    return (_mix_and_ffn(x_prompt, proj_p, proj_s[6:], *rest),
            _mix_and_ffn(x_sample, proj_s[:6], proj_s[6:], *rest))
```

```python
import functools
import math

import jax
import jax.numpy as jnp
from jax import lax
from jax.experimental import pallas as pl
from jax.experimental.pallas import tpu as pltpu

D_MODEL = 2048
MLA_HEADS = 8
MLA_NOPE = 128
MLA_ROPE = 64
MLA_V = 128
KV_RANK = 512
ROPE_THETA = 10000.0
SWA_HEADS = 8
SWA_KV_HEADS = 2
SWA_HEAD_DIM = 128
SWA_GROUP = SWA_HEADS // SWA_KV_HEADS
WINDOW = 128
BLOCK = 128
D_FF = -(-8 * D_MODEL // (3 * 256)) * 256
EPS = 1e-6

N_MLA_Q = MLA_HEADS * (MLA_NOPE + MLA_ROPE)
N_SWA_Q = SWA_HEADS * SWA_HEAD_DIM
N_SWA_KV = SWA_KV_HEADS * SWA_HEAD_DIM
MLA_OUT = MLA_HEADS * MLA_V
SWA_OUT = SWA_HEADS * SWA_HEAD_DIM

LANES = 128
QK_PAD = 2 * LANES
OFF_QPE = MLA_HEADS * MLA_NOPE
OFF_QS = OFF_QPE + MLA_HEADS * MLA_ROPE
OFF_KS = OFF_QS + N_SWA_Q
OFF_VS = OFF_KS + N_SWA_KV
OFF_CKV = OFF_VS + N_SWA_KV
OFF_KPE = OFF_CKV + KV_RANK
N_PROJ = OFF_KPE + LANES

LOG2E = math.log2(math.e)
MLA_SCALE = (MLA_NOPE + MLA_ROPE) ** -0.5
SWA_SCALE = SWA_HEAD_DIM ** -0.5

VMEM_LIMIT = 56 * 1024 * 1024

TR_W1 = 256
TM_PROJ = 512
TQ_MLA = 256
QT_MLA = 2
BF16_SUBLANES = 16
ONES_ROWS = BF16_SUBLANES
V_ROWS = MLA_V + ONES_ROWS
CK_MLA = 512
TQ_SWA = 2048
TM_OUT = 512
TM_FFN = 1024
TF_FFN = 512
SUB_FFN = 512

_NT = (((1,), (1,)), ((), ()))


def _params(n_axes):
    return pltpu.CompilerParams(
        dimension_semantics=("arbitrary",) * n_axes, vmem_limit_bytes=VMEM_LIMIT)


def _resident(shape):
    return pl.BlockSpec(shape, lambda *_: (0,) * len(shape), pipeline_mode=pl.Buffered(1))


def _rms(x, g):
    var = jnp.mean(x * x, axis=-1, keepdims=True)
    return x * lax.rsqrt(var + EPS) * g


def _rope_pairs(x, cos_t, sin_a, sin_b):
    half = MLA_ROPE // 2
    return x * cos_t + pltpu.roll(x, LANES - half, 1) * sin_a + pltpu.roll(x, half, 1) * sin_b


def _proj_kernel(*refs, n_cast):
    (x_ref, g_ref, w1_ref, gckv_ref, wuk_ref, wuvt_ref, cos_ref, sina_ref, sinb_ref) = refs[:9]
    q_ref, k_ref, vt_ref, qs_ref, ks_ref, vst_ref = refs[9 + n_cast:15 + n_cast]
    for src_ref, dst_ref in zip(refs[9:9 + n_cast], refs[15 + n_cast:]):
        dst_ref[...] = src_ref[...].astype(dst_ref.dtype)

    h = _rms(x_ref[...], g_ref[...]).astype(jnp.bfloat16)
    cos_t, sin_a, sin_b = cos_ref[...], sina_ref[...], sinb_ref[...]
    bf = jnp.bfloat16

    def proj(lo, hi):
        return jnp.dot(h, w1_ref[:, lo:hi], preferred_element_type=jnp.float32)

    low_half = lax.broadcasted_iota(jnp.int32, cos_t.shape, 1) < MLA_ROPE
    q_scale = MLA_SCALE * LOG2E
    q_pe = proj(OFF_QPE, OFF_QS) * q_scale
    for pair in range(MLA_HEADS // 2):
        nope = proj(pair * 2 * MLA_NOPE, (pair + 1) * 2 * MLA_NOPE) * q_scale
        rot = _rope_pairs(q_pe[:, pair * LANES:(pair + 1) * LANES], cos_t, sin_a, sin_b)
        q_ref[2 * pair, :, :LANES] = nope[:, :MLA_NOPE].astype(bf)
        q_ref[2 * pair, :, LANES:] = jnp.where(low_half, rot, 0.0).astype(bf)
        q_ref[2 * pair + 1, :, :LANES] = nope[:, MLA_NOPE:].astype(bf)
        q_ref[2 * pair + 1, :, LANES:] = jnp.where(low_half, pltpu.roll(rot, MLA_ROPE, 1), 0.0).astype(bf)

    qs_ref[...] = (proj(OFF_QS, OFF_KS) * (SWA_SCALE * LOG2E)).astype(bf)
    kvs = proj(OFF_KS, OFF_CKV)
    ks_ref[...] = kvs[:, :N_SWA_KV].astype(bf)
    vst_ref[...] = kvs[:, N_SWA_KV:].T.astype(bf)

    lat = proj(OFF_CKV, N_PROJ)
    c_kv = _rms(lat[:, :KV_RANK], gckv_ref[...]).astype(bf)
    kpe = _rope_pairs(lat[:, KV_RANK:], cos_t, sin_a, sin_b).astype(bf)
    k_nope = jnp.dot(c_kv, wuk_ref[...], preferred_element_type=jnp.float32)
    for hd in range(MLA_HEADS):
        k_ref[hd, :, :LANES] = k_nope[:, hd * MLA_NOPE:(hd + 1) * MLA_NOPE].astype(bf)
        k_ref[hd, :, LANES:] = kpe
    vt = lax.dot_general(wuvt_ref[...], c_kv, _NT, preferred_element_type=jnp.float32)
    for hd in range(MLA_HEADS):
        vt_ref[hd, :MLA_V, :] = vt[hd * MLA_V:(hd + 1) * MLA_V, :].astype(bf)
        vt_ref[hd, MLA_V:, :] = jnp.ones((ONES_ROWS, vt.shape[1]), bf)


def _cast_spec(shape, steps):
    rows, cols = shape
    group = 1
    while (rows * group) % steps or (rows * group // steps) % BF16_SUBLANES:
        group *= 2
    return pl.BlockSpec((rows * group // steps, cols), lambda i: (i // group, 0))


def _projection(x2d, B, S, g_pre, w1, g_ckv, w_uk, w_uvt, cos_t, sin_a, sin_b, cast=()):
    T = B * S
    tm = TM_PROJ
    spt = S // tm
    row = lambda i: (i, 0)
    tab = pl.BlockSpec((tm, LANES), lambda i: (i % spt, 0))
    tposed = lambda n: pl.BlockSpec((None, n, tm), lambda i: (i // spt, 0, i % spt))
    bf = jnp.bfloat16
    cast_specs = [_cast_spec(c.shape, T // tm) for c in cast]
    return pl.pallas_call(
        functools.partial(_proj_kernel, n_cast=len(cast)),
        grid=(T // tm,),
        in_specs=[
            pl.BlockSpec((tm, D_MODEL), row),
            _resident((1, D_MODEL)),
            _resident((D_MODEL, N_PROJ)),
            _resident((1, KV_RANK)),
            _resident((KV_RANK, MLA_HEADS * MLA_NOPE)),
            _resident((MLA_OUT, KV_RANK)),
            tab, tab, tab,
        ] + cast_specs,
        out_specs=[
            pl.BlockSpec((MLA_HEADS, tm, QK_PAD), lambda i: (0, i, 0)),
            pl.BlockSpec((MLA_HEADS, tm, QK_PAD), lambda i: (0, i, 0)),
            pl.BlockSpec((None, MLA_HEADS, V_ROWS, tm), lambda i: (i // spt, 0, 0, i % spt)),
            pl.BlockSpec((tm, N_SWA_Q), row),
            pl.BlockSpec((tm, N_SWA_KV), row),
            tposed(N_SWA_KV),
        ] + cast_specs,
        out_shape=[
            jax.ShapeDtypeStruct((MLA_HEADS, T, QK_PAD), bf),
            jax.ShapeDtypeStruct((MLA_HEADS, T, QK_PAD), bf),
            jax.ShapeDtypeStruct((B, MLA_HEADS, V_ROWS, S), bf),
            jax.ShapeDtypeStruct((T, N_SWA_Q), bf),
            jax.ShapeDtypeStruct((T, N_SWA_KV), bf),
            jax.ShapeDtypeStruct((B, N_SWA_KV, S), bf),
        ] + [jax.ShapeDtypeStruct(c.shape, bf) for c in cast],
        compiler_params=_params(1),
        name="proj",
    )(x2d, g_pre, w1, g_ckv, w_uk, w_uvt, cos_t, sin_a, sin_b, *cast)


def _mla_kernel(q_ref, k_ref, vt_ref, o_ref, sa_ref, sb_ref):
    seq = k_ref.shape[1]
    tq = sa_ref.shape[1]
    n_units = MLA_HEADS * (q_ref.shape[1] // tq)
    chunks = [slice(c * CK_MLA, (c + 1) * CK_MLA) for c in range(seq // CK_MLA)]

    def unit(u):
        hd, qt = u % MLA_HEADS, u // MLA_HEADS
        return hd, slice(qt * tq, (qt + 1) * tq)

    def scores(u, s_ref):
        hd, rows = unit(u)
        q = q_ref[hd, rows, :]
        m = None
        for ck in chunks:
            s = lax.dot_general(k_ref[hd, ck, :], q, _NT, preferred_element_type=jnp.float32)
            s_ref[ck, :] = s
            cm = jnp.max(s, axis=0, keepdims=True)
            m = cm if m is None else jnp.maximum(m, cm)
        return m

    def attend(u, s_ref, m):
        hd, rows = unit(u)
        acc = jnp.zeros((V_ROWS, tq), jnp.float32)
        for ck in chunks:
            p = jnp.exp2(s_ref[ck, :] - m)
            acc = acc + jnp.dot(vt_ref[hd, :, ck], p.astype(jnp.bfloat16),
                                preferred_element_type=jnp.float32)
        o_ref[hd, rows, :] = (acc[:MLA_V] / acc[MLA_V:MLA_V + 1]).T.astype(o_ref.dtype)

    bufs = (sa_ref, sb_ref)
    m_prev = scores(0, bufs[0])
    for u in range(1, n_units):
        m_cur = scores(u, bufs[u % 2])
        attend(u - 1, bufs[(u - 1) % 2], m_prev)
        m_prev = m_cur
    attend(n_units - 1, bufs[(n_units - 1) % 2], m_prev)


def _mla_attention(q, k, vt, B, S):
    T = B * S
    rows = TQ_MLA * QT_MLA
    nq = S // rows
    kv_bytes = S * MLA_HEADS * (QK_PAD + V_ROWS) * 2
    mode = dict(pipeline_mode=pl.Buffered(1)) if 2 * kv_bytes > VMEM_LIMIT // 2 else {}
    return pl.pallas_call(
        _mla_kernel,
        grid=(B, nq),
        in_specs=[
            pl.BlockSpec((MLA_HEADS, rows, QK_PAD), lambda b, i: (0, b * nq + i, 0)),
            pl.BlockSpec((MLA_HEADS, S, QK_PAD), lambda b, i: (0, b, 0), **mode),
            pl.BlockSpec((None, MLA_HEADS, V_ROWS, S), lambda b, i: (b, 0, 0, 0), **mode),
        ],
        out_specs=pl.BlockSpec((MLA_HEADS, rows, MLA_V), lambda b, i: (0, b * nq + i, 0)),
        out_shape=jax.ShapeDtypeStruct((MLA_HEADS, T, MLA_V), jnp.bfloat16),
        scratch_shapes=[pltpu.VMEM((S, TQ_MLA), jnp.float32), pltpu.VMEM((S, TQ_MLA), jnp.float32)],
        compiler_params=_params(2),
        name="mla",
    )(q, k, vt)


def _swa_kernel(sink_ref, bias_ref, q_ref, kp_ref, kc_ref, kn_ref, vp_ref, vc_ref, vn_ref, o_ref,
                sa_ref, sb_ref, *, seq):
    i = pl.program_id(1)
    tq = q_ref.shape[0]
    nblk = tq // BLOCK
    kcat = jnp.concatenate([kp_ref[...], kc_ref[...], kn_ref[...]], axis=0)
    vcat = jnp.concatenate([vp_ref[...], vc_ref[...], vn_ref[...]], axis=1)
    slot = lax.broadcasted_iota(jnp.int32, (3 * BLOCK, 1), 0)
    ones = jnp.ones((ONES_ROWS, 3 * BLOCK), jnp.bfloat16)
    sinks = [jnp.concatenate(
        [jnp.full((1, BLOCK), sink_ref[kv * SWA_GROUP + g] * LOG2E, jnp.float32)
         for g in range(SWA_GROUP)], axis=1) for kv in range(SWA_KV_HEADS)]

    def unit(u):
        kv, jb = u // nblk, u % nblk
        dsl = slice(kv * SWA_HEAD_DIM, (kv + 1) * SWA_HEAD_DIM)
        win = slice(jb * BLOCK, (jb + 3) * BLOCK)
        return kv, jb, dsl, win, slice(jb * BLOCK, (jb + 1) * BLOCK)

    def scores(u, s_ref):
        kv, jb, dsl, win, qrows = unit(u)
        qg = jnp.concatenate(
            [q_ref[qrows, (kv * SWA_GROUP + g) * SWA_HEAD_DIM:(kv * SWA_GROUP + g + 1) * SWA_HEAD_DIM]
             for g in range(SWA_GROUP)], axis=0)
        s = lax.dot_general(kcat[win, dsl], qg, _NT, preferred_element_type=jnp.float32)
        s = s + bias_ref[kv]
        if jb == 0 or jb == nblk - 1:
            key_pos = i * tq + (jb - 1) * BLOCK + slot
            s = jnp.where((key_pos >= 0) & (key_pos < seq), s, -jnp.inf)
        s_ref[...] = s
        return jnp.maximum(jnp.max(s, axis=0, keepdims=True), sinks[kv])

    def attend(u, s_ref, m):
        kv, jb, dsl, win, qrows = unit(u)
        p = jnp.exp2(s_ref[...] - m).astype(jnp.bfloat16)
        acc = jnp.dot(jnp.concatenate([vcat[dsl, win], ones], axis=0), p,
                      preferred_element_type=jnp.float32)
        denom = acc[SWA_HEAD_DIM:SWA_HEAD_DIM + 1] + jnp.exp2(sinks[kv] - m)
        ot = acc[:SWA_HEAD_DIM] / denom
        for g in range(SWA_GROUP):
            hd = kv * SWA_GROUP + g
            o_ref[qrows, hd * SWA_HEAD_DIM:(hd + 1) * SWA_HEAD_DIM] = (
                ot[:, g * BLOCK:(g + 1) * BLOCK].T.astype(o_ref.dtype))

    n_units = SWA_KV_HEADS * nblk
    bufs = (sa_ref, sb_ref)
    m_prev = scores(0, bufs[0])
    for u in range(1, n_units):
        m_cur = scores(u, bufs[u % 2])
        attend(u - 1, bufs[(u - 1) % 2], m_prev)
        m_prev = m_cur
    attend(n_units - 1, bufs[(n_units - 1) % 2], m_prev)


def _swa_bias():
    r = jnp.arange(3 * BLOCK)[:, None]
    qi = jnp.arange(BLOCK)[None, :]
    dist = jnp.abs(qi + BLOCK - r)
    slopes = 2.0 ** (-8.0 * jnp.arange(1, SWA_HEADS + 1, dtype=jnp.float32) / SWA_HEADS)
    b = jnp.where(dist <= WINDOW, -(slopes * LOG2E)[:, None, None] * dist.astype(jnp.float32),
                  -jnp.inf)
    b = b.reshape(SWA_KV_HEADS, SWA_GROUP, 3 * BLOCK, BLOCK).transpose(0, 2, 1, 3)
    return b.reshape(SWA_KV_HEADS, 3 * BLOCK, SWA_GROUP * BLOCK)


def _swa_attention(sink, qs, ks, vst, B, S):
    T = B * S
    tq = TQ_SWA
    nt = S // tq
    per = tq // BLOCK
    nb = S // BLOCK
    prev = lambda j: jnp.maximum(j * per - 1, 0)
    nxt = lambda j: jnp.minimum((j + 1) * per, nb - 1)
    return pl.pallas_call(
        functools.partial(_swa_kernel, seq=S),
        grid=(B, nt),
        in_specs=[
            pl.BlockSpec(memory_space=pltpu.SMEM),
            _resident((SWA_KV_HEADS, 3 * BLOCK, SWA_GROUP * BLOCK)),
            pl.BlockSpec((tq, N_SWA_Q), lambda b, j: (b * nt + j, 0)),
            pl.BlockSpec((BLOCK, N_SWA_KV), lambda b, j: (b * nb + prev(j), 0)),
            pl.BlockSpec((tq, N_SWA_KV), lambda b, j: (b * nt + j, 0)),
            pl.BlockSpec((BLOCK, N_SWA_KV), lambda b, j: (b * nb + nxt(j), 0)),
            pl.BlockSpec((None, N_SWA_KV, BLOCK), lambda b, j: (b, 0, prev(j))),
            pl.BlockSpec((None, N_SWA_KV, tq), lambda b, j: (b, 0, j)),
            pl.BlockSpec((None, N_SWA_KV, BLOCK), lambda b, j: (b, 0, nxt(j))),
        ],
        out_specs=pl.BlockSpec((tq, SWA_OUT), lambda b, j: (b * nt + j, 0)),
        out_shape=jax.ShapeDtypeStruct((T, SWA_OUT), jnp.bfloat16),
        scratch_shapes=[pltpu.VMEM((3 * BLOCK, SWA_GROUP * BLOCK), jnp.float32)] * 2,
        compiler_params=_params(2),
        name="swa",
    )(sink, _swa_bias(), qs, ks, ks, ks, vst, vst, vst)


def _out_kernel(oa_ref, ob_ref, x_ref, ga_ref, gb_ref, wo_ref, gpost_ref, x1_ref):
    oa = jnp.concatenate([oa_ref[hd] for hd in range(MLA_HEADS)], axis=1)
    na = _rms(oa.astype(jnp.float32), ga_ref[...]).astype(jnp.bfloat16)
    o = jnp.dot(na, wo_ref[:MLA_OUT, :], preferred_element_type=jnp.float32)
    nb = _rms(ob_ref[...].astype(jnp.float32), gb_ref[...]).astype(jnp.bfloat16)
    o = o + jnp.dot(nb, wo_ref[MLA_OUT:, :], preferred_element_type=jnp.float32)
    x1_ref[...] = x_ref[...] + _rms(o, gpost_ref[...])


def _out_projection(oa, ob, x2d, g_a, g_b, w_o, g_post):
    T = x2d.shape[0]
    tm = TM_OUT
    row = lambda i: (i, 0)
    return pl.pallas_call(
        _out_kernel,
        grid=(T // tm,),
        in_specs=[
            pl.BlockSpec((MLA_HEADS, tm, MLA_V), lambda i: (0, i, 0)),
            pl.BlockSpec((tm, SWA_OUT), row),
            pl.BlockSpec((tm, D_MODEL), row),
            _resident((1, MLA_OUT)),
            _resident((1, SWA_OUT)),
            _resident((MLA_OUT + SWA_OUT, D_MODEL)),
            _resident((1, D_MODEL)),
        ],
        out_specs=pl.BlockSpec((tm, D_MODEL), row),
        out_shape=jax.ShapeDtypeStruct((T, D_MODEL), jnp.float32),
        compiler_params=_params(1),
        name="outproj",
    )(oa, ob, x2d, g_a, g_b, w_o, g_post)


def _ffn_kernel(x1_ref, gpre_ref, wg_ref, wu_ref, wd_ref, gpost_ref, y_ref, h_ref):
    j = pl.program_id(1)
    last_j = pl.num_programs(1) - 1

    def step(first, last):
        sub = SUB_FFN if (first or last) else h_ref.shape[0]
        for r0 in range(0, h_ref.shape[0], sub):
            rows = slice(r0, r0 + sub)
            if first:
                h = _rms(x1_ref[rows, :], gpre_ref[...]).astype(h_ref.dtype)
                h_ref[rows, :] = h
            else:
                h = h_ref[rows, :]
            gate = jnp.dot(h, wg_ref[...], preferred_element_type=jnp.float32)
            up = jnp.dot(h, wu_ref[...], preferred_element_type=jnp.float32)
            act = (gate * jax.nn.sigmoid(gate) * up).astype(jnp.bfloat16)
            if first or last:
                acc = jnp.dot(act, wd_ref[...], preferred_element_type=jnp.float32)
                if not first:
                    acc = y_ref[rows, :] + acc
                if last:
                    acc = x1_ref[rows, :] + _rms(acc, gpost_ref[...])
                y_ref[rows, :] = acc
            else:
                for c0 in range(0, D_MODEL, D_MODEL // 2):
                    cols = slice(c0, c0 + D_MODEL // 2)
                    y_ref[rows, cols] += jnp.dot(act, wd_ref[:, cols],
                                                 preferred_element_type=jnp.float32)

    pl.when(j == 0)(functools.partial(step, True, False))
    pl.when((j > 0) & (j < last_j))(functools.partial(step, False, False))
    pl.when(j == last_j)(functools.partial(step, False, True))


def _ffn(x1, g_pre, w_gate, w_up, w_down, g_post):
    T = x1.shape[0]
    tm, tf = TM_FFN, TF_FFN
    assert D_FF // tf >= 2, "first and last d_ff chunk must be different steps"
    row = lambda i, j: (i, 0)
    return pl.pallas_call(
        _ffn_kernel,
        grid=(T // tm, D_FF // tf),
        in_specs=[
            pl.BlockSpec((tm, D_MODEL), row),
            _resident((1, D_MODEL)),
            pl.BlockSpec((D_MODEL, tf), lambda i, j: (0, j)),
            pl.BlockSpec((D_MODEL, tf), lambda i, j: (0, j)),
            pl.BlockSpec((tf, D_MODEL), lambda i, j: (j, 0)),
            _resident((1, D_MODEL)),
        ],
        out_specs=pl.BlockSpec((tm, D_MODEL), row),
        out_shape=jax.ShapeDtypeStruct((T, D_MODEL), jnp.float32),
        scratch_shapes=[pltpu.VMEM((tm, D_MODEL), jnp.bfloat16)],
        compiler_params=_params(2),
        name="ffn",
    )(x1, g_pre, w_gate, w_up, w_down, g_post)


def _rope_tables(S):
    inv = ROPE_THETA ** (-jnp.arange(0, MLA_ROPE, 2, dtype=jnp.float32) / MLA_ROPE)
    ang = jnp.arange(S, dtype=jnp.float32)[:, None] * inv[None, :]
    c, s = jnp.cos(ang), jnp.sin(ang)
    z = jnp.zeros_like(c)
    cos_t = jnp.concatenate([c, c, c, c], axis=1)
    sin_a = jnp.concatenate([-s, z, -s, z], axis=1)
    sin_b = jnp.concatenate([z, s, z, s], axis=1)
    return cos_t, sin_a, sin_b


def _w1_kernel(w_ref, o_ref):
    w = w_ref[...]
    hw = MLA_NOPE + MLA_ROPE
    nope = [w[:, hd * hw:hd * hw + MLA_NOPE] for hd in range(MLA_HEADS)]
    qpe = [w[:, hd * hw + MLA_NOPE:(hd + 1) * hw] for hd in range(MLA_HEADS)]
    o = N_MLA_Q
    ckv = w[:, o:o + KV_RANK]
    kpe = w[:, o + KV_RANK:o + KV_RANK + MLA_ROPE]
    swa = w[:, o + KV_RANK + MLA_ROPE:]
    pad = jnp.zeros((w.shape[0], LANES - MLA_ROPE), w.dtype)
    o_ref[...] = jnp.concatenate(nope + qpe + [swa, ckv, kpe, pad], axis=1).astype(o_ref.dtype)


def _permute_w_in(w_in):
    tr = TR_W1
    return pl.pallas_call(
        _w1_kernel,
        grid=(D_MODEL // tr,),
        in_specs=[pl.BlockSpec((tr, w_in.shape[1]), lambda i: (i, 0))],
        out_specs=pl.BlockSpec((tr, N_PROJ), lambda i: (i, 0)),
        out_shape=jax.ShapeDtypeStruct((D_MODEL, N_PROJ), jnp.bfloat16),
        compiler_params=_params(1),
        name="w1prep",
    )(w_in)


def _prep_weights(w_in, w_uk, w_uv):
    bf = jnp.bfloat16
    return dict(
        w1=_permute_w_in(w_in),
        w_uk=w_uk.reshape(KV_RANK, MLA_HEADS * MLA_NOPE).astype(bf),
        w_uvt=w_uv.reshape(KV_RANK, MLA_OUT).T.astype(bf))


def _project(x, w, g_pre_attn, g_ckv, cast=()):
    B, S, _ = x.shape
    return _projection(x.reshape(B * S, D_MODEL), B, S, g_pre_attn, w["w1"], g_ckv, w["w_uk"],
                       w["w_uvt"], *_rope_tables(S), cast=cast)


def _mix_and_ffn(x, proj, late_w, sink, g_out_mla, g_out_swa, g_post_attn, g_pre_ffn, g_post_ffn):
    B, S, _ = x.shape
    q, k, vt, qs, ks, vst = proj
    w_o, w_gate, w_up, w_down = late_w
    oa = _mla_attention(q, k, vt, B, S)
    ob = _swa_attention(sink, qs, ks, vst, B, S)
    x1 = _out_projection(oa, ob, x.reshape(B * S, D_MODEL), g_out_mla, g_out_swa, w_o, g_post_attn)
    return _ffn(x1, g_pre_ffn, w_gate, w_up, w_down, g_post_ffn).reshape(B, S, D_MODEL)


def kernel(x_prompt, x_sample, g_pre_attn, w_in, g_ckv, w_uk, w_uv, sink, g_out_mla, g_out_swa,
           w_o, g_post_attn, g_pre_ffn, w_gate, w_up, w_down, g_post_ffn):
    assert w_in.shape[0] == 1, "single layer"
    w = _prep_weights(w_in[0], w_uk[0], w_uv[0])
    proj_s = _project(x_sample, w, g_pre_attn, g_ckv,
                      cast=(w_o[0], w_gate[0], w_up[0], w_down[0]))
    proj_p = _project(x_prompt, w, g_pre_attn, g_ckv)
    rest = (sink[0], g_out_mla, g_out_swa, g_post_attn, g_pre_ffn, g_post_ffn)
    return (_mix_and_ffn(x_prompt, proj_p, proj_s[6:], *rest),
            _mix_and_ffn(x_sample, proj_s[:6], proj_s[6:], *rest))
```
